```python
import jax, jax.numpy as jnp
from jax import lax
import numpy as np

D_MODEL = 1024
BATCH = 8
SEQ = 2048
DEPTH = 2
DEC_BATCH = 128
DEC_SEQ = 4
PAST_LEN = 16384
PAGE_SIZE = 128

WIDTH_A = D_MODEL
WIDTH_B = D_MODEL
LRU_HEADS = 16
LRU_BLOCK = WIDTH_B // LRU_HEADS
CONV_A_WIDTH = 3
CONV_B_WIDTH = 4
LRU_C = 8.0
PLE_DIM = 256
RMS_EPS = 1e-6
SPLIT_SIZES = [WIDTH_A, WIDTH_A, WIDTH_A, WIDTH_A, WIDTH_B, WIDTH_B, D_MODEL, D_MODEL]
IN_COLS = sum(SPLIT_SIZES)
SPLIT_POINTS = [int(v) for v in np.cumsum(SPLIT_SIZES)[:-1]]

kernel_name = "hybrid_gconv_rglru_parallel_step"


def _rmsnorm(x, g):
    x32 = x.astype(jnp.float32)
    y = x32 * lax.rsqrt(jnp.mean(x32 * x32, axis=-1, keepdims=True) + RMS_EPS)
    return (y * g.astype(jnp.float32)).astype(x.dtype)


def _causal_dwconv(x, buf, w):
    k_width = w.shape[0]
    t_len = x.shape[1]
    xp = jnp.concatenate([buf.astype(x.dtype), x], axis=1)
    y = xp[:, 0:t_len] * w[0]
    for k in range(1, k_width):
        y = y + xp[:, k:k + t_len] * w[k]
    return y, xp[:, t_len:]


def _rg_lru(xc, h0, w_r, b_r, w_i, b_i, lam):
    bsz, t_len, width = xc.shape
    x32 = xc.astype(jnp.float32)
    xh = x32.reshape(bsz, t_len, LRU_HEADS, LRU_BLOCK)
    r = jax.nn.sigmoid(jnp.einsum('bthi,hij->bthj', xh, w_r.astype(jnp.float32)).reshape(bsz, t_len, width) + b_r.astype(jnp.float32))
    gi = jax.nn.sigmoid(jnp.einsum('bthi,hij->bthj', xh, w_i.astype(jnp.float32)).reshape(bsz, t_len, width) + b_i.astype(jnp.float32))
    log_a = -LRU_C * r * jax.nn.softplus(-lam.astype(jnp.float32))
    a = jnp.exp(log_a)
    beta = jnp.sqrt(-jnp.expm1(2.0 * log_a))
    u = beta * gi * x32

    def step(h, inp):
        a_t, u_t = inp
        h = a_t * h + u_t
        return h, h

    h_last, hs = lax.scan(step, h0.astype(jnp.float32), (jnp.swapaxes(a, 0, 1), jnp.swapaxes(u, 0, 1)))
    return jnp.swapaxes(hs, 0, 1).astype(xc.dtype), h_last


def _layer(x, p, buf_a, buf_b, h0, norm_in, w_in, conv_a_w, conv_b_w, conv_b_b,
           w_r, b_r, w_i, b_i, lam, w_a_out, w_b_out, w_o, norm_pe, w_pg, w_pe):
    hn = _rmsnorm(x, norm_in)
    z = hn @ w_in
    u_a, b_a, c_a, g_a, x_b, g_b, m_a, m_b = jnp.split(z, SPLIT_POINTS, axis=-1)
    conv_out, new_a = _causal_dwconv(c_a * u_a, buf_a, conv_a_w)
    y_a = b_a * conv_out * jax.nn.silu(g_a)
    xc, new_b = _causal_dwconv(x_b, buf_b, conv_b_w)
    xc = xc + conv_b_b
    hs, h_last = _rg_lru(xc, h0, w_r, b_r, w_i, b_i, lam)
    y_b = hs * jax.nn.silu(g_b)
    merged = jax.nn.sigmoid(m_a) * (y_a @ w_a_out) + jax.nn.sigmoid(m_b) * (y_b @ w_b_out)
    x = x + merged @ w_o
    gate = jax.nn.sigmoid(_rmsnorm(x, norm_pe) @ w_pg)
    x = x + (p @ w_pe) * gate
    return x, new_a, new_b, h_last


def setup_inputs(seed: int = 0) -> dict:
    key = jax.random.key(seed)
    ks = jax.random.split(key, 24)
    nrm = jax.random.normal
    f32 = jnp.float32
    a0 = jax.random.uniform(ks[18], (DEPTH, WIDTH_B), f32, minval=0.9, maxval=0.999)
    return {
        "x_prompt": nrm(ks[0], (BATCH, SEQ, D_MODEL), f32),
        "x_sample": nrm(ks[1], (DEC_BATCH, DEC_SEQ, D_MODEL), f32),
        "state_conv_a": nrm(ks[2], (DEPTH, DEC_BATCH, CONV_A_WIDTH - 1, WIDTH_A), f32),
        "state_conv_b": nrm(ks[3], (DEPTH, DEC_BATCH, CONV_B_WIDTH - 1, WIDTH_B), f32),
        "state_h": 0.5 * nrm(ks[4], (DEPTH, DEC_BATCH, WIDTH_B), f32),
        "p_prompt": nrm(ks[5], (DEPTH, BATCH, SEQ, PLE_DIM), f32),
        "p_sample": nrm(ks[6], (DEPTH, DEC_BATCH, DEC_SEQ, PLE_DIM), f32),
        "norm_in": 1.0 + 0.05 * nrm(ks[7], (DEPTH, D_MODEL), f32),
        "w_in": nrm(ks[8], (DEPTH, D_MODEL, IN_COLS), f32) * D_MODEL ** -0.5,
        "conv_a_w": nrm(ks[9], (DEPTH, CONV_A_WIDTH, WIDTH_A), f32) * CONV_A_WIDTH ** -0.5,
        "conv_b_w": nrm(ks[10], (DEPTH, CONV_B_WIDTH, WIDTH_B), f32) * CONV_B_WIDTH ** -0.5,
        "conv_b_b": 0.02 * nrm(ks[11], (DEPTH, WIDTH_B), f32),
        "w_r": nrm(ks[12], (DEPTH, LRU_HEADS, LRU_BLOCK, LRU_BLOCK), f32) * LRU_BLOCK ** -0.5,
        "b_r": 0.02 * nrm(ks[13], (DEPTH, WIDTH_B), f32),
        "w_i": nrm(ks[14], (DEPTH, LRU_HEADS, LRU_BLOCK, LRU_BLOCK), f32) * LRU_BLOCK ** -0.5,
        "b_i": 0.02 * nrm(ks[15], (DEPTH, WIDTH_B), f32),
        "lam": jnp.log(a0) - jnp.log1p(-a0),
        "w_a_out": nrm(ks[16], (DEPTH, WIDTH_A, D_MODEL), f32) * WIDTH_A ** -0.5,
        "w_b_out": nrm(ks[17], (DEPTH, WIDTH_B, D_MODEL), f32) * WIDTH_B ** -0.5,
        "w_o": nrm(ks[19], (DEPTH, D_MODEL, D_MODEL), f32) * D_MODEL ** -0.5,
        "norm_pe": 1.0 + 0.05 * nrm(ks[20], (DEPTH, D_MODEL), f32),
        "w_pg": nrm(ks[21], (DEPTH, D_MODEL, D_MODEL), f32) * D_MODEL ** -0.5,
        "w_pe": nrm(ks[22], (DEPTH, PLE_DIM, D_MODEL), f32) * PLE_DIM ** -0.5,
        "norm_final": 1.0 + 0.05 * nrm(ks[23], (D_MODEL,), f32),
    }


def reference(x_prompt, x_sample, state_conv_a, state_conv_b, state_h, p_prompt, p_sample,
              norm_in, w_in, conv_a_w, conv_b_w, conv_b_b, w_r, b_r, w_i, b_i, lam,
              w_a_out, w_b_out, w_o, norm_pe, w_pg, w_pe, norm_final):
    xp, xs = x_prompt, x_sample
    n_p = x_prompt.shape[0]
    ca_p, cb_p, h_p, ca_s, cb_s, h_s = [], [], [], [], [], []
    for l in range(DEPTH):
        weights = (norm_in[l], w_in[l], conv_a_w[l], conv_b_w[l], conv_b_b[l], w_r[l], b_r[l],
                   w_i[l], b_i[l], lam[l], w_a_out[l], w_b_out[l], w_o[l], norm_pe[l], w_pg[l], w_pe[l])
        buf_a0 = jnp.zeros((n_p, CONV_A_WIDTH - 1, WIDTH_A), xp.dtype)
        buf_b0 = jnp.zeros((n_p, CONV_B_WIDTH - 1, WIDTH_B), xp.dtype)
        h00 = jnp.zeros((n_p, WIDTH_B), jnp.float32)
        xp, na, nb, nh = _layer(xp, p_prompt[l], buf_a0, buf_b0, h00, *weights)
        ca_p.append(na); cb_p.append(nb); h_p.append(nh)
        xs, na, nb, nh = _layer(xs, p_sample[l], state_conv_a[l], state_conv_b[l], state_h[l], *weights)
        ca_s.append(na); cb_s.append(nb); h_s.append(nh)
    y_prompt = _rmsnorm(xp, norm_final)
    y_sample = _rmsnorm(xs, norm_final)
    return (y_prompt, y_sample,
            jnp.stack(ca_p), jnp.stack(cb_p), jnp.stack(h_p),
            jnp.stack(ca_s), jnp.stack(cb_s), jnp.stack(h_s))
```

```python
import functools

import jax
import jax.numpy as jnp
from jax import lax
from jax.experimental import pallas as pl
from jax.experimental.pallas import tpu as pltpu

D_MODEL = 1024
WIDTH = 1024
PLE_DIM = 256
LRU_HEADS = 16
LRU_BLOCK = WIDTH // LRU_HEADS
LRU_C = 8.0
RMS_EPS = 1e-6
CONV_A_WIDTH = 3
CONV_B_WIDTH = 4

MXU_DIM = 256
N_GROUPS = WIDTH // MXU_DIM
SUBLANES = 8
TILE_ROWS = 512
CHUNK = 16
Z_COLS = 4 * WIDTH

V_NORM_IN, V_CONV_A, V_CONV_B, V_CONV_B_BIAS = 0, 1, 4, 8
V_B_R, V_B_I, V_LAM, V_NORM_PE, V_NORM_FINAL = 9, 10, 11, 12, 13
V_ROWS = 16

VMEM_LIMIT_BYTES = 58 * 1024 * 1024


def _sigmoid(x):
    return 0.5 * jnp.tanh(0.5 * x) + 0.5


def _silu(x):
    h = 0.5 * x
    return h + h * jnp.tanh(h)


def _softplus(y):
    return jnp.maximum(y, 0.0) + jnp.log1p(jnp.exp(-jnp.abs(y)))


def _rms_scale(x, g):
    ms = jnp.mean(x * x, axis=-1, keepdims=True)
    return x * lax.rsqrt(ms + RMS_EPS) * g


def _for_chunks(n_rows, chunk, body):
    def step(i, carry):
        body(pl.multiple_of(i * chunk, chunk))
        return carry
    lax.fori_loop(0, n_rows // chunk, step, 0)


def _layer_kernel(x_ref, p_ref, bufa0_ref, bufb0_ref, h0_ref, vec_ref,
                  w_in_ref, w_ri_ref, w_ao_ref, w_bo_ref, w_o_ref, w_pg_ref, w_pe_ref,
                  y_ref, newa_ref, newb_ref, hlast_ref,
                  hn_ref, z_ref, exta_ref, extb_ref, h_ref, ya_ref, yb_ref,
                  *, nb, final_norm):
    rows = x_ref.shape[0]
    tt = rows // nb
    hist_a = (CONV_A_WIDTH - 1) * nb
    hist_b = (CONV_B_WIDTH - 1) * nb
    f32, bf16 = jnp.float32, jnp.bfloat16

    @pl.when(pl.program_id(0) == 0)
    def _():
        exta_ref[0:hist_a, :] = bufa0_ref[...]
        extb_ref[0:hist_b, :] = bufb0_ref[...]
        h_ref[...] = h0_ref[...]

    def vec(row, n=1):
        return vec_ref[row:row + n, :]

    g_in = vec(V_NORM_IN)

    def norm_in_body(r0):
        sl = pl.ds(r0, CHUNK)
        hn_ref[sl, :] = _rms_scale(x_ref[sl, :], g_in).astype(bf16)
    _for_chunks(rows, CHUNK, norm_in_body)

    z_ref[...] = jnp.dot(hn_ref[...], w_in_ref[:, 0:4 * WIDTH], preferred_element_type=f32)
    wa = [vec(V_CONV_A + k) for k in range(CONV_A_WIDTH)]

    def branch_a_body(r0):
        sl = pl.ds(r0, CHUNK)
        v = z_ref[sl, 2 * WIDTH:3 * WIDTH] * z_ref[sl, 0:WIDTH]
        exta_ref[pl.ds(r0 + hist_a, CHUNK), :] = v
        conv = v * wa[CONV_A_WIDTH - 1]
        for k in range(CONV_A_WIDTH - 1):
            conv = conv + exta_ref[pl.ds(r0 + k * nb, CHUNK), :] * wa[k]
        ya = z_ref[sl, WIDTH:2 * WIDTH] * conv * _silu(z_ref[sl, 3 * WIDTH:4 * WIDTH])
        ya_ref[sl, :] = ya.astype(bf16)
    _for_chunks(rows, CHUNK, branch_a_body)

    z_ref[:, 0:2 * WIDTH] = jnp.dot(hn_ref[...], w_in_ref[:, 4 * WIDTH:6 * WIDTH],
                                    preferred_element_type=f32)
    wb = [vec(V_CONV_B + k) for k in range(CONV_B_WIDTH)]
    bias_b = vec(V_CONV_B_BIAS)
    hn2_ref = yb_ref

    def conv_b_body(r0):
        sl = pl.ds(r0, CHUNK)
        xb = z_ref[sl, 0:WIDTH]
        extb_ref[pl.ds(r0 + hist_b, CHUNK), :] = xb
        xc = xb * wb[CONV_B_WIDTH - 1] + bias_b
        for k in range(CONV_B_WIDTH - 1):
            xc = xc + extb_ref[pl.ds(r0 + k * nb, CHUNK), :] * wb[k]
        z_ref[sl, 0:WIDTH] = xc
        hn2_ref[sl, :] = xc.astype(bf16)
    _for_chunks(rows, CHUNK, conv_b_body)

    for j in range(N_GROUPS):
        z_ref[:, 2 * WIDTH + 2 * MXU_DIM * j:2 * WIDTH + 2 * MXU_DIM * (j + 1)] = jnp.dot(
            hn2_ref[:, MXU_DIM * j:MXU_DIM * (j + 1)], w_ri_ref[j], preferred_element_type=f32)

    neg_c_sp = -LRU_C * _softplus(-vec(V_LAM))
    b_r, b_i = vec(V_B_R), vec(V_B_I)

    def gates_body(r0):
        sl = pl.ds(r0, CHUNK)
        for j in range(N_GROUPS):
            cols = slice(MXU_DIM * j, MXU_DIM * (j + 1))
            rcols = slice(2 * WIDTH + 2 * MXU_DIM * j, 2 * WIDTH + 2 * MXU_DIM * j + MXU_DIM)
            icols = slice(2 * WIDTH + 2 * MXU_DIM * j + MXU_DIM, 2 * WIDTH + 2 * MXU_DIM * (j + 1))
            r = _sigmoid(z_ref[sl, rcols] + b_r[:, cols])
            gi = _sigmoid(z_ref[sl, icols] + b_i[:, cols])
            log_a = r * neg_c_sp[:, cols]
            a = jnp.exp(log_a)
            beta = jnp.sqrt(jnp.tanh(-log_a) * (1.0 + a * a))
            z_ref[sl, rcols] = a
            z_ref[sl, icols] = beta * gi * z_ref[sl, cols]
    _for_chunks(rows, CHUNK, gates_body)

    def scan_body(bc, carry):
        b0 = pl.multiple_of(bc * SUBLANES, SUBLANES)
        hs = tuple(h_ref[pl.ds(b0, SUBLANES), MXU_DIM * j:MXU_DIM * (j + 1)] for j in range(N_GROUPS))

        def t_body(t, hs):
            rsl = pl.ds(pl.multiple_of(t * nb + b0, SUBLANES), SUBLANES)
            out = []
            for j in range(N_GROUPS):
                acols = slice(2 * WIDTH + 2 * MXU_DIM * j, 2 * WIDTH + 2 * MXU_DIM * j + MXU_DIM)
                ucols = slice(2 * WIDTH + 2 * MXU_DIM * j + MXU_DIM, 2 * WIDTH + 2 * MXU_DIM * (j + 1))
                h = z_ref[rsl, acols] * hs[j] + z_ref[rsl, ucols]
                z_ref[rsl, ucols] = h
                out.append(h)
            return tuple(out)
        hs = lax.fori_loop(0, tt, t_body, hs, unroll=min(tt, 8))
        for j in range(N_GROUPS):
            h_ref[pl.ds(b0, SUBLANES), MXU_DIM * j:MXU_DIM * (j + 1)] = hs[j]
        return carry
    lax.fori_loop(0, nb // SUBLANES, scan_body, 0)

    def yb_body(r0):
        sl = pl.ds(r0, CHUNK)
        sg = _silu(z_ref[sl, WIDTH:2 * WIDTH])
        for j in range(N_GROUPS):
            cols = slice(MXU_DIM * j, MXU_DIM * (j + 1))
            hcols = slice(2 * WIDTH + 2 * MXU_DIM * j + MXU_DIM, 2 * WIDTH + 2 * MXU_DIM * (j + 1))
            yb_ref[sl, cols] = (z_ref[sl, hcols] * sg[:, cols]).astype(bf16)
    _for_chunks(rows, CHUNK, yb_body)

    z_ref[:, 0:2 * WIDTH] = jnp.dot(hn_ref[...], w_in_ref[:, 6 * WIDTH:8 * WIDTH],
                                    preferred_element_type=f32)
    z_ref[:, 2 * WIDTH:3 * WIDTH] = jnp.dot(ya_ref[...], w_ao_ref[...], preferred_element_type=f32)
    z_ref[:, 3 * WIDTH:4 * WIDTH] = jnp.dot(yb_ref[...], w_bo_ref[...], preferred_element_type=f32)

    def merge_body(r0):
        sl = pl.ds(r0, CHUNK)
        m = (_sigmoid(z_ref[sl, 0:WIDTH]) * z_ref[sl, 2 * WIDTH:3 * WIDTH]
             + _sigmoid(z_ref[sl, WIDTH:2 * WIDTH]) * z_ref[sl, 3 * WIDTH:4 * WIDTH])
        ya_ref[sl, :] = m.astype(bf16)
    _for_chunks(rows, CHUNK, merge_body)

    z_ref[:, 0:WIDTH] = jnp.dot(ya_ref[...], w_o_ref[...], preferred_element_type=f32)
    g_pe = vec(V_NORM_PE)

    def resid_body(r0):
        sl = pl.ds(r0, CHUNK)
        x1 = x_ref[sl, :] + z_ref[sl, 0:WIDTH]
        y_ref[sl, :] = x1
        hn_ref[sl, :] = _rms_scale(x1, g_pe).astype(bf16)
    _for_chunks(rows, CHUNK, resid_body)

    z_ref[:, 0:WIDTH] = jnp.dot(hn_ref[...], w_pg_ref[...], preferred_element_type=f32)
    z_ref[:, WIDTH:2 * WIDTH] = jnp.dot(p_ref[...].astype(bf16), w_pe_ref[...],
                                        preferred_element_type=f32)
    g_fin = vec(V_NORM_FINAL)

    def embed_body(r0):
        sl = pl.ds(r0, CHUNK)
        x2 = y_ref[sl, :] + z_ref[sl, WIDTH:2 * WIDTH] * _sigmoid(z_ref[sl, 0:WIDTH])
        y_ref[sl, :] = _rms_scale(x2, g_fin) if final_norm else x2
    _for_chunks(rows, CHUNK, embed_body)

    newa_ref[...] = exta_ref[rows:rows + hist_a, :]
    newb_ref[...] = extb_ref[rows:rows + hist_b, :]
    hlast_ref[...] = h_ref[...]
    exta_ref[0:hist_a, :] = exta_ref[rows:rows + hist_a, :]
    extb_ref[0:hist_b, :] = extb_ref[rows:rows + hist_b, :]


def _run_layer(x, p, bufa0, bufb0, h0, vecs, weights, *, nb, final_norm):
    n_rows = x.shape[0]
    assert n_rows % TILE_ROWS == 0 and TILE_ROWS % nb == 0 and nb % SUBLANES == 0
    assert TILE_ROWS >= (CONV_B_WIDTH - 1) * nb
    hist_a = (CONV_A_WIDTH - 1) * nb
    hist_b = (CONV_B_WIDTH - 1) * nb
    f32, bf16 = jnp.float32, jnp.bfloat16

    row_spec = lambda cols: pl.BlockSpec((TILE_ROWS, cols), lambda i: (i, 0))
    const_spec = lambda shape: pl.BlockSpec(shape, lambda i: (0,) * len(shape))
    resident = pl.BlockSpec(memory_space=pltpu.VMEM)

    return pl.pallas_call(
        functools.partial(_layer_kernel, nb=nb, final_norm=final_norm),
        grid=(n_rows // TILE_ROWS,),
        in_specs=[row_spec(D_MODEL), row_spec(PLE_DIM),
                  const_spec((hist_a, WIDTH)), const_spec((hist_b, WIDTH)), const_spec((nb, WIDTH)),
                  const_spec((V_ROWS, WIDTH))] + [resident] * len(weights),
        out_specs=[row_spec(D_MODEL), const_spec((hist_a, WIDTH)), const_spec((hist_b, WIDTH)),
                   const_spec((nb, WIDTH))],
        out_shape=[jax.ShapeDtypeStruct((n_rows, D_MODEL), f32),
                   jax.ShapeDtypeStruct((hist_a, WIDTH), f32),
                   jax.ShapeDtypeStruct((hist_b, WIDTH), f32),
                   jax.ShapeDtypeStruct((nb, WIDTH), f32)],
        scratch_shapes=[pltpu.VMEM((TILE_ROWS, D_MODEL), bf16),
                        pltpu.VMEM((TILE_ROWS, Z_COLS), f32),
                        pltpu.VMEM((TILE_ROWS + hist_a, WIDTH), f32),
                        pltpu.VMEM((TILE_ROWS + hist_b, WIDTH), f32),
                        pltpu.VMEM((nb, WIDTH), f32),
                        pltpu.VMEM((TILE_ROWS, WIDTH), bf16),
                        pltpu.VMEM((TILE_ROWS, WIDTH), bf16)],
        compiler_params=pltpu.CompilerParams(dimension_semantics=("arbitrary",),
                                             vmem_limit_bytes=VMEM_LIMIT_BYTES),
        name="hybrid_layer_final" if final_norm else "hybrid_layer",
    )(x, p, bufa0, bufb0, h0, vecs, *weights)


def _gate_weights(w_r, w_i):
    per = MXU_DIM // LRU_BLOCK

    def superblocks(w):
        w = w.reshape(N_GROUPS, per, LRU_BLOCK, LRU_BLOCK)
        eye = jnp.eye(per, dtype=w.dtype)
        return jnp.einsum('gaij,ab->gaibj', w, eye).reshape(N_GROUPS, MXU_DIM, MXU_DIM)
    return jnp.concatenate([superblocks(w_r), superblocks(w_i)], axis=-1)


def _time_major(a):
    b, t, c = a.shape
    return jnp.swapaxes(a, 0, 1).reshape(t * b, c)


def _batch_major(a, nb):
    return jnp.swapaxes(a.reshape(a.shape[0] // nb, nb, a.shape[1]), 0, 1)


def kernel(x_prompt, x_sample, state_conv_a, state_conv_b, state_h, p_prompt, p_sample, norm_in, w_in, conv_a_w, conv_b_w, conv_b_b, w_r, b_r, w_i, b_i, lam, w_a_out, w_b_out, w_o, norm_pe, w_pg, w_pe, norm_final):
    depth = w_in.shape[0]
    n_p, n_s = x_prompt.shape[0], x_sample.shape[0]
    f32, bf16 = jnp.float32, jnp.bfloat16

    xp, xs = _time_major(x_prompt), _time_major(x_sample)
    zeros_a = jnp.zeros(((CONV_A_WIDTH - 1) * n_p, WIDTH), f32)
    zeros_b = jnp.zeros(((CONV_B_WIDTH - 1) * n_p, WIDTH), f32)
    zeros_h = jnp.zeros((n_p, WIDTH), f32)
    outs = {k: [] for k in ("ca_p", "cb_p", "h_p", "ca_s", "cb_s", "h_s")}
    for l in range(depth):
        final = l == depth - 1
        vecs = jnp.concatenate([
            norm_in[l][None], conv_a_w[l], conv_b_w[l], conv_b_b[l][None], b_r[l][None],
            b_i[l][None], lam[l][None], norm_pe[l][None], norm_final[None],
            jnp.zeros((V_ROWS - 14, WIDTH), f32)], axis=0)
        weights = (w_in[l].astype(bf16), _gate_weights(w_r[l], w_i[l]).astype(bf16),
                   w_a_out[l].astype(bf16), w_b_out[l].astype(bf16), w_o[l].astype(bf16),
                   w_pg[l].astype(bf16), w_pe[l].astype(bf16))
        xp, na, nb_, nh = _run_layer(xp, _time_major(p_prompt[l]), zeros_a, zeros_b, zeros_h,
                                     vecs, weights, nb=n_p, final_norm=final)
        outs["ca_p"].append(_batch_major(na, n_p))
        outs["cb_p"].append(_batch_major(nb_, n_p))
        outs["h_p"].append(nh)
        xs, na, nb_, nh = _run_layer(xs, _time_major(p_sample[l]), _time_major(state_conv_a[l]),
                                     _time_major(state_conv_b[l]), state_h[l],
                                     vecs, weights, nb=n_s, final_norm=final)
        outs["ca_s"].append(_batch_major(na, n_s))
        outs["cb_s"].append(_batch_major(nb_, n_s))
        outs["h_s"].append(nh)
    return (_batch_major(xp, n_p), _batch_major(xs, n_s),
            jnp.stack(outs["ca_p"]), jnp.stack(outs["cb_p"]), jnp.stack(outs["h_p"]),
            jnp.stack(outs["ca_s"]), jnp.stack(outs["cb_s"]), jnp.stack(outs["h_s"]))
```

```python
import functools

import jax
import jax.numpy as jnp
from jax import lax
from jax.experimental import pallas as pl
from jax.experimental.pallas import tpu as pltpu

D_MODEL = 1024
WIDTH = 1024
PLE_DIM = 256
LRU_HEADS = 16
LRU_BLOCK = WIDTH // LRU_HEADS
LRU_C = 8.0
RMS_EPS = 1e-6
CONV_A_WIDTH = 3
CONV_B_WIDTH = 4

MXU_DIM = 256
N_GROUPS = WIDTH // MXU_DIM
SUBLANES = 8
TILE_ROWS = 512
CHUNK = 16
Z_COLS = 4 * WIDTH

V_NORM_IN, V_CONV_A, V_CONV_B, V_CONV_B_BIAS = 0, 1, 4, 8
V_B_R, V_B_I, V_LAM, V_NORM_PE, V_NORM_FINAL = 9, 10, 11, 12, 13
V_ROWS = 16

VMEM_LIMIT_BYTES = 62 * 1024 * 1024


def _sigmoid(x):
    return 0.5 * jnp.tanh(0.5 * x) + 0.5


def _silu(x):
    h = 0.5 * x
    return h + h * jnp.tanh(h)


def _softplus(y):
    return jnp.maximum(y, 0.0) + jnp.log1p(jnp.exp(-jnp.abs(y)))


def _rms_scale(x, g):
    ms = jnp.mean(x * x, axis=-1, keepdims=True)
    return x * lax.rsqrt(ms + RMS_EPS) * g


def _for_chunks(n_rows, chunk, body):
    for i in range(n_rows // chunk):
        body(i * chunk)


def _layer_kernel(x_ref, p_ref, bufa0_ref, bufb0_ref, h0_ref, vec_ref,
                  w_in_ref, w_ri_ref, w_ao_ref, w_bo_ref, w_o_ref, w_pg_ref, w_pe_ref,
                  y_ref, newa_ref, newb_ref, hlast_ref,
                  hn_ref, z_ref, exta_ref, extb_ref, h_ref, ya_ref, yb_ref,
                  *, nb, final_norm):
    rows = x_ref.shape[0]
    tt = rows // nb
    hist_a = (CONV_A_WIDTH - 1) * nb
    hist_b = (CONV_B_WIDTH - 1) * nb
    f32, bf16 = jnp.float32, jnp.bfloat16

    @pl.when(pl.program_id(0) == 0)
    def _():
        exta_ref[0:hist_a, :] = bufa0_ref[...]
        extb_ref[0:hist_b, :] = bufb0_ref[...]
        h_ref[...] = h0_ref[...]

    def vec(row, n=1):
        return vec_ref[row:row + n, :]

    g_in = vec(V_NORM_IN)

    def norm_in_body(r0):
        sl = pl.ds(r0, CHUNK)
        hn_ref[sl, :] = _rms_scale(x_ref[sl, :], g_in).astype(bf16)
    _for_chunks(rows, CHUNK, norm_in_body)

    z_ref[...] = jnp.dot(hn_ref[...], w_in_ref[:, 0:4 * WIDTH], preferred_element_type=f32)
    wa = [vec(V_CONV_A + k) for k in range(CONV_A_WIDTH)]

    def branch_a_body(r0):
        sl = pl.ds(r0, CHUNK)
        v = z_ref[sl, 2 * WIDTH:3 * WIDTH] * z_ref[sl, 0:WIDTH]
        exta_ref[pl.ds(r0 + hist_a, CHUNK), :] = v
        conv = v * wa[CONV_A_WIDTH - 1]
        for k in range(CONV_A_WIDTH - 1):
            conv = conv + exta_ref[pl.ds(r0 + k * nb, CHUNK), :] * wa[k]
        ya = z_ref[sl, WIDTH:2 * WIDTH] * conv * _silu(z_ref[sl, 3 * WIDTH:4 * WIDTH])
        ya_ref[sl, :] = ya.astype(bf16)
    _for_chunks(rows, CHUNK, branch_a_body)

    z_ref[:, 0:2 * WIDTH] = jnp.dot(hn_ref[...], w_in_ref[:, 4 * WIDTH:6 * WIDTH],
                                    preferred_element_type=f32)
    wb = [vec(V_CONV_B + k) for k in range(CONV_B_WIDTH)]
    bias_b = vec(V_CONV_B_BIAS)
    hn2_ref = yb_ref

    def conv_b_body(r0):
        sl = pl.ds(r0, CHUNK)
        xb = z_ref[sl, 0:WIDTH]
        extb_ref[pl.ds(r0 + hist_b, CHUNK), :] = xb
        xc = xb * wb[CONV_B_WIDTH - 1] + bias_b
        for k in range(CONV_B_WIDTH - 1):
            xc = xc + extb_ref[pl.ds(r0 + k * nb, CHUNK), :] * wb[k]
        z_ref[sl, 0:WIDTH] = xc
        hn2_ref[sl, :] = xc.astype(bf16)
    _for_chunks(rows, CHUNK, conv_b_body)

    for j in range(N_GROUPS):
        z_ref[:, 2 * WIDTH + 2 * MXU_DIM * j:2 * WIDTH + 2 * MXU_DIM * (j + 1)] = jnp.dot(
            hn2_ref[:, MXU_DIM * j:MXU_DIM * (j + 1)], w_ri_ref[j], preferred_element_type=f32)

    neg_c_sp = -LRU_C * _softplus(-vec(V_LAM))
    b_r, b_i = vec(V_B_R), vec(V_B_I)

    def gates_body(r0):
        sl = pl.ds(r0, CHUNK)
        for j in range(N_GROUPS):
            cols = slice(MXU_DIM * j, MXU_DIM * (j + 1))
            rcols = slice(2 * WIDTH + 2 * MXU_DIM * j, 2 * WIDTH + 2 * MXU_DIM * j + MXU_DIM)
            icols = slice(2 * WIDTH + 2 * MXU_DIM * j + MXU_DIM, 2 * WIDTH + 2 * MXU_DIM * (j + 1))
            r = _sigmoid(z_ref[sl, rcols] + b_r[:, cols])
            gi = _sigmoid(z_ref[sl, icols] + b_i[:, cols])
            log_a = r * neg_c_sp[:, cols]
            a = jnp.exp(log_a)
            beta = jnp.sqrt(jnp.tanh(-log_a) * (1.0 + a * a))
            z_ref[sl, rcols] = a
            z_ref[sl, icols] = beta * gi * z_ref[sl, cols]
    _for_chunks(rows, CHUNK, gates_body)

    def scan_body(bc, carry):
        b0 = pl.multiple_of(bc * SUBLANES, SUBLANES)
        hs = tuple(h_ref[pl.ds(b0, SUBLANES), MXU_DIM * j:MXU_DIM * (j + 1)] for j in range(N_GROUPS))

        def t_body(t, hs):
            rsl = pl.ds(pl.multiple_of(t * nb + b0, SUBLANES), SUBLANES)
            out = []
            for j in range(N_GROUPS):
                acols = slice(2 * WIDTH + 2 * MXU_DIM * j, 2 * WIDTH + 2 * MXU_DIM * j + MXU_DIM)
                ucols = slice(2 * WIDTH + 2 * MXU_DIM * j + MXU_DIM, 2 * WIDTH + 2 * MXU_DIM * (j + 1))
                h = z_ref[rsl, acols] * hs[j] + z_ref[rsl, ucols]
                z_ref[rsl, ucols] = h
                out.append(h)
            return tuple(out)
        hs = lax.fori_loop(0, tt, t_body, hs, unroll=min(tt, 8))
        for j in range(N_GROUPS):
            h_ref[pl.ds(b0, SUBLANES), MXU_DIM * j:MXU_DIM * (j + 1)] = hs[j]
        return carry
    lax.fori_loop(0, nb // SUBLANES, scan_body, 0)

    def yb_body(r0):
        sl = pl.ds(r0, CHUNK)
        sg = _silu(z_ref[sl, WIDTH:2 * WIDTH])
        for j in range(N_GROUPS):
            cols = slice(MXU_DIM * j, MXU_DIM * (j + 1))
            hcols = slice(2 * WIDTH + 2 * MXU_DIM * j + MXU_DIM, 2 * WIDTH + 2 * MXU_DIM * (j + 1))
            yb_ref[sl, cols] = (z_ref[sl, hcols] * sg[:, cols]).astype(bf16)
    _for_chunks(rows, CHUNK, yb_body)

    z_ref[:, 0:2 * WIDTH] = jnp.dot(hn_ref[...], w_in_ref[:, 6 * WIDTH:8 * WIDTH],
                                    preferred_element_type=f32)
    z_ref[:, 2 * WIDTH:3 * WIDTH] = jnp.dot(ya_ref[...], w_ao_ref[...], preferred_element_type=f32)
    z_ref[:, 3 * WIDTH:4 * WIDTH] = jnp.dot(yb_ref[...], w_bo_ref[...], preferred_element_type=f32)

    def merge_body(r0):
        sl = pl.ds(r0, CHUNK)
        m = (_sigmoid(z_ref[sl, 0:WIDTH]) * z_ref[sl, 2 * WIDTH:3 * WIDTH]
             + _sigmoid(z_ref[sl, WIDTH:2 * WIDTH]) * z_ref[sl, 3 * WIDTH:4 * WIDTH])
        ya_ref[sl, :] = m.astype(bf16)
    _for_chunks(rows, CHUNK, merge_body)

    z_ref[:, 0:WIDTH] = jnp.dot(ya_ref[...], w_o_ref[...], preferred_element_type=f32)
    g_pe = vec(V_NORM_PE)

    def resid_body(r0):
        sl = pl.ds(r0, CHUNK)
        x1 = x_ref[sl, :] + z_ref[sl, 0:WIDTH]
        y_ref[sl, :] = x1
        hn_ref[sl, :] = _rms_scale(x1, g_pe).astype(bf16)
    _for_chunks(rows, CHUNK, resid_body)

    z_ref[:, 0:WIDTH] = jnp.dot(hn_ref[...], w_pg_ref[...], preferred_element_type=f32)
    z_ref[:, WIDTH:2 * WIDTH] = jnp.dot(p_ref[...].astype(bf16), w_pe_ref[...],
                                        preferred_element_type=f32)
    g_fin = vec(V_NORM_FINAL)

    def embed_body(r0):
        sl = pl.ds(r0, CHUNK)
        x2 = y_ref[sl, :] + z_ref[sl, WIDTH:2 * WIDTH] * _sigmoid(z_ref[sl, 0:WIDTH])
        y_ref[sl, :] = _rms_scale(x2, g_fin) if final_norm else x2
    _for_chunks(rows, CHUNK, embed_body)

    newa_ref[...] = exta_ref[rows:rows + hist_a, :]
    newb_ref[...] = extb_ref[rows:rows + hist_b, :]
    hlast_ref[...] = h_ref[...]
    exta_ref[0:hist_a, :] = exta_ref[rows:rows + hist_a, :]
    extb_ref[0:hist_b, :] = extb_ref[rows:rows + hist_b, :]


def _run_layer(x, p, bufa0, bufb0, h0, vecs, weights, *, nb, final_norm):
    n_rows = x.shape[0]
    assert n_rows % TILE_ROWS == 0 and TILE_ROWS % nb == 0 and nb % SUBLANES == 0
    assert TILE_ROWS >= (CONV_B_WIDTH - 1) * nb
    hist_a = (CONV_A_WIDTH - 1) * nb
    hist_b = (CONV_B_WIDTH - 1) * nb
    f32, bf16 = jnp.float32, jnp.bfloat16

    row_spec = lambda cols: pl.BlockSpec((TILE_ROWS, cols), lambda i: (i, 0))
    const_spec = lambda shape: pl.BlockSpec(shape, lambda i: (0,) * len(shape))
    resident = pl.BlockSpec(memory_space=pltpu.VMEM)

    return pl.pallas_call(
        functools.partial(_layer_kernel, nb=nb, final_norm=final_norm),
        grid=(n_rows // TILE_ROWS,),
        in_specs=[row_spec(D_MODEL), row_spec(PLE_DIM),
                  const_spec((hist_a, WIDTH)), const_spec((hist_b, WIDTH)), const_spec((nb, WIDTH)),
                  const_spec((V_ROWS, WIDTH))] + [resident] * len(weights),
        out_specs=[row_spec(D_MODEL), const_spec((hist_a, WIDTH)), const_spec((hist_b, WIDTH)),
                   const_spec((nb, WIDTH))],
        out_shape=[jax.ShapeDtypeStruct((n_rows, D_MODEL), f32),
                   jax.ShapeDtypeStruct((hist_a, WIDTH), f32),
                   jax.ShapeDtypeStruct((hist_b, WIDTH), f32),
                   jax.ShapeDtypeStruct((nb, WIDTH), f32)],
        scratch_shapes=[pltpu.VMEM((TILE_ROWS, D_MODEL), bf16),
                        pltpu.VMEM((TILE_ROWS, Z_COLS), f32),
                        pltpu.VMEM((TILE_ROWS + hist_a, WIDTH), f32),
                        pltpu.VMEM((TILE_ROWS + hist_b, WIDTH), f32),
                        pltpu.VMEM((nb, WIDTH), f32),
                        pltpu.VMEM((TILE_ROWS, WIDTH), bf16),
                        pltpu.VMEM((TILE_ROWS, WIDTH), bf16)],
        compiler_params=pltpu.CompilerParams(dimension_semantics=("arbitrary",),
                                             vmem_limit_bytes=VMEM_LIMIT_BYTES),
        name="hybrid_layer_final" if final_norm else "hybrid_layer",
    )(x, p, bufa0, bufb0, h0, vecs, *weights)


def _gate_weights(w_r, w_i):
    per = MXU_DIM // LRU_BLOCK

    def superblocks(w):
        w = w.reshape(N_GROUPS, per, LRU_BLOCK, LRU_BLOCK)
        eye = jnp.eye(per, dtype=w.dtype)
        return jnp.einsum('gaij,ab->gaibj', w, eye).reshape(N_GROUPS, MXU_DIM, MXU_DIM)
    return jnp.concatenate([superblocks(w_r), superblocks(w_i)], axis=-1)


def _time_major(a):
    b, t, c = a.shape
    return jnp.swapaxes(a, 0, 1).reshape(t * b, c)


def _batch_major(a, nb):
    return jnp.swapaxes(a.reshape(a.shape[0] // nb, nb, a.shape[1]), 0, 1)


def kernel(x_prompt, x_sample, state_conv_a, state_conv_b, state_h, p_prompt, p_sample, norm_in, w_in, conv_a_w, conv_b_w, conv_b_b, w_r, b_r, w_i, b_i, lam, w_a_out, w_b_out, w_o, norm_pe, w_pg, w_pe, norm_final):
    depth = w_in.shape[0]
    n_p, n_s = x_prompt.shape[0], x_sample.shape[0]
    f32, bf16 = jnp.float32, jnp.bfloat16

    xp, xs = _time_major(x_prompt), _time_major(x_sample)
    zeros_a = jnp.zeros(((CONV_A_WIDTH - 1) * n_p, WIDTH), f32)
    zeros_b = jnp.zeros(((CONV_B_WIDTH - 1) * n_p, WIDTH), f32)
    zeros_h = jnp.zeros((n_p, WIDTH), f32)
    outs = {k: [] for k in ("ca_p", "cb_p", "h_p", "ca_s", "cb_s", "h_s")}
    for l in range(depth):
        final = l == depth - 1
        vecs = jnp.concatenate([
            norm_in[l][None], conv_a_w[l], conv_b_w[l], conv_b_b[l][None], b_r[l][None],
            b_i[l][None], lam[l][None], norm_pe[l][None], norm_final[None],
            jnp.zeros((V_ROWS - 14, WIDTH), f32)], axis=0)
        weights = (w_in[l].astype(bf16), _gate_weights(w_r[l], w_i[l]).astype(bf16),
                   w_a_out[l].astype(bf16), w_b_out[l].astype(bf16), w_o[l].astype(bf16),
                   w_pg[l].astype(bf16), w_pe[l].astype(bf16))
        xp, na, nb_, nh = _run_layer(xp, _time_major(p_prompt[l]), zeros_a, zeros_b, zeros_h,
                                     vecs, weights, nb=n_p, final_norm=final)
        outs["ca_p"].append(_batch_major(na, n_p))
        outs["cb_p"].append(_batch_major(nb_, n_p))
        outs["h_p"].append(nh)
        xs, na, nb_, nh = _run_layer(xs, _time_major(p_sample[l]), _time_major(state_conv_a[l]),
                                     _time_major(state_conv_b[l]), state_h[l],
                                     vecs, weights, nb=n_s, final_norm=final)
        outs["ca_s"].append(_batch_major(na, n_s))
        outs["cb_s"].append(_batch_major(nb_, n_s))
        outs["h_s"].append(nh)
    return (_batch_major(xp, n_p), _batch_major(xs, n_s),
            jnp.stack(outs["ca_p"]), jnp.stack(outs["cb_p"]), jnp.stack(outs["h_p"]),
            jnp.stack(outs["ca_s"]), jnp.stack(outs["cb_s"]), jnp.stack(outs["h_s"]))
```

```python
import functools

import jax
import jax.numpy as jnp
from jax import lax
from jax.experimental import pallas as pl
from jax.experimental.pallas import tpu as pltpu

D_MODEL = 1024
WIDTH = 1024
PLE_DIM = 256
LRU_HEADS = 16
LRU_BLOCK = WIDTH // LRU_HEADS
LRU_C = 8.0
RMS_EPS = 1e-6
CONV_A_WIDTH = 3
CONV_B_WIDTH = 4

MXU_DIM = 256
N_GROUPS = WIDTH // MXU_DIM
SUBLANES = 8
TILE_ROWS = 512
CHUNK = 16
Z_COLS = 4 * WIDTH
N_SLOTS = 2

V_NORM_IN, V_CONV_A, V_CONV_B, V_CONV_B_BIAS = 0, 1, 4, 8
V_B_R, V_B_I, V_LAM, V_NORM_PE, V_NORM_FINAL = 9, 10, 11, 12, 13
V_ROWS = 16

VMEM_LIMIT_BYTES = 62 * 1024 * 1024


def _sigmoid(x):
    return 0.5 * jnp.tanh(0.5 * x) + 0.5


def _silu(x):
    h = 0.5 * x
    return h + h * jnp.tanh(h)


def _softplus(y):
    return jnp.maximum(y, 0.0) + jnp.log1p(jnp.exp(-jnp.abs(y)))


def _rms_scale(x, g):
    ms = jnp.mean(x * x, axis=-1, keepdims=True)
    return x * lax.rsqrt(ms + RMS_EPS) * g


def _chunks(n_rows):
    return range(0, n_rows, CHUNK)


def _layer_kernel(x_in, p_in, bufa0_ref, bufb0_ref, h0_ref, vec_ref,
                  w_in_ref, w_ri_ref, w_ao_ref, w_bo_ref, w_o_ref, w_pg_ref, w_pe_ref,
                  y_out, newa_ref, newb_ref, hlast_ref,
                  hn_ref, z_ref, exta_ref, extb_ref, h_ref, ya_ref, yb_ref, *io_scratch,
                  nb, layer, final_norm, batch_major_io):
    rows = TILE_ROWS
    tt = rows // nb
    hist_a = (CONV_A_WIDTH - 1) * nb
    hist_b = (CONV_B_WIDTH - 1) * nb
    f32, bf16 = jnp.float32, jnp.bfloat16
    step = pl.program_id(0)
    n_steps = pl.num_programs(0)

    if batch_major_io:
        xbuf, pbuf, ybuf, sem_x, sem_p, sem_y = io_scratch
        slot = step % N_SLOTS

        def in_copies(s, sl):
            t0 = s * tt
            dst = pl.ds(sl * tt, tt)
            cps = []
            for b in range(nb):
                cps.append(pltpu.make_async_copy(x_in.at[b, pl.ds(t0, tt), :],
                                                 xbuf.at[dst, b, :], sem_x.at[sl]))
                cps.append(pltpu.make_async_copy(p_in.at[layer, b, pl.ds(t0, tt), :],
                                                 pbuf.at[dst, b, :], sem_p.at[sl]))
            return cps

        def out_copies(s, sl):
            src = pl.ds(sl * tt, tt)
            return [pltpu.make_async_copy(ybuf.at[src, b, :], y_out.at[b, pl.ds(s * tt, tt), :],
                                          sem_y.at[sl]) for b in range(nb)]

        @pl.when(step == 0)
        def _():
            for c in in_copies(0, 0):
                c.start()

        @pl.when(step + 1 < n_steps)
        def _():
            for c in in_copies(step + 1, (step + 1) % N_SLOTS):
                c.start()

        for c in in_copies(step, slot):
            c.wait()

        @pl.when(step >= N_SLOTS)
        def _():
            for c in out_copies(step - N_SLOTS, slot):
                c.wait()

        def staged(buf, r0, n):
            return pl.ds(slot * tt + r0 // nb, n // nb)

        def load_x(r0, n):
            return xbuf[staged(xbuf, r0, n), :, :].reshape(n, D_MODEL)

        def load_p(r0, n):
            return pbuf[staged(pbuf, r0, n), :, :].reshape(n, PLE_DIM)

        def load_y(r0, n):
            return ybuf[staged(ybuf, r0, n), :, :].reshape(n, D_MODEL)

        def store_y(r0, n, val):
            ybuf[staged(ybuf, r0, n), :, :] = val.reshape(n // nb, nb, D_MODEL)
    else:
        def load_x(r0, n):
            return x_in[pl.ds(r0, n), :]

        def load_p(r0, n):
            return p_in[pl.ds(r0, n), :]

        def load_y(r0, n):
            return y_out[pl.ds(r0, n), :]

        def store_y(r0, n, val):
            y_out[pl.ds(r0, n), :] = val

    @pl.when(step == 0)
    def _():
        exta_ref[0:hist_a, :] = bufa0_ref[...]
        extb_ref[0:hist_b, :] = bufb0_ref[...]
        h_ref[...] = h0_ref[...]

    def vec(row, n=1):
        return vec_ref[row:row + n, :]

    g_in = vec(V_NORM_IN)
    for r0 in _chunks(rows):
        hn_ref[pl.ds(r0, CHUNK), :] = _rms_scale(load_x(r0, CHUNK), g_in).astype(bf16)

    z_ref[...] = jnp.dot(hn_ref[...], w_in_ref[:, 0:4 * WIDTH], preferred_element_type=f32)
    wa = [vec(V_CONV_A + k) for k in range(CONV_A_WIDTH)]
    for r0 in _chunks(rows):
        sl = pl.ds(r0, CHUNK)
        v = z_ref[sl, 2 * WIDTH:3 * WIDTH] * z_ref[sl, 0:WIDTH]
        exta_ref[pl.ds(r0 + hist_a, CHUNK), :] = v
        conv = v * wa[CONV_A_WIDTH - 1]
        for k in range(CONV_A_WIDTH - 1):
            conv = conv + exta_ref[pl.ds(r0 + k * nb, CHUNK), :] * wa[k]
        ya = z_ref[sl, WIDTH:2 * WIDTH] * conv * _silu(z_ref[sl, 3 * WIDTH:4 * WIDTH])
        ya_ref[sl, :] = ya.astype(bf16)

    z_ref[:, 0:2 * WIDTH] = jnp.dot(hn_ref[...], w_in_ref[:, 4 * WIDTH:6 * WIDTH],
                                    preferred_element_type=f32)
    wb = [vec(V_CONV_B + k) for k in range(CONV_B_WIDTH)]
    bias_b = vec(V_CONV_B_BIAS)
    xcb_ref = yb_ref
    for r0 in _chunks(rows):
        sl = pl.ds(r0, CHUNK)
        xb = z_ref[sl, 0:WIDTH]
        extb_ref[pl.ds(r0 + hist_b, CHUNK), :] = xb
        xc = xb * wb[CONV_B_WIDTH - 1] + bias_b
        for k in range(CONV_B_WIDTH - 1):
            xc = xc + extb_ref[pl.ds(r0 + k * nb, CHUNK), :] * wb[k]
        z_ref[sl, 0:WIDTH] = xc
        xcb_ref[sl, :] = xc.astype(bf16)

    def gate_cols(j):
        c0 = 2 * WIDTH + 2 * MXU_DIM * j
        return slice(c0, c0 + MXU_DIM), slice(c0 + MXU_DIM, c0 + 2 * MXU_DIM)

    for j in range(N_GROUPS):
        z_ref[:, 2 * WIDTH + 2 * MXU_DIM * j:2 * WIDTH + 2 * MXU_DIM * (j + 1)] = jnp.dot(
            xcb_ref[:, MXU_DIM * j:MXU_DIM * (j + 1)], w_ri_ref[j], preferred_element_type=f32)

    neg_c_sp = -LRU_C * _softplus(-vec(V_LAM))
    b_r, b_i = vec(V_B_R), vec(V_B_I)
    for r0 in _chunks(rows):
        sl = pl.ds(r0, CHUNK)
        for j in range(N_GROUPS):
            cols = slice(MXU_DIM * j, MXU_DIM * (j + 1))
            rcols, icols = gate_cols(j)
            r = _sigmoid(z_ref[sl, rcols] + b_r[:, cols])
            gi = _sigmoid(z_ref[sl, icols] + b_i[:, cols])
            log_a = r * neg_c_sp[:, cols]
            a = jnp.exp(log_a)
            beta = jnp.sqrt(jnp.tanh(-log_a) * (1.0 + a * a))
            z_ref[sl, rcols] = a
            z_ref[sl, icols] = beta * gi * z_ref[sl, cols]

    def scan_body(bc, carry):
        b0 = pl.multiple_of(bc * SUBLANES, SUBLANES)
        hs = tuple(h_ref[pl.ds(b0, SUBLANES), MXU_DIM * j:MXU_DIM * (j + 1)] for j in range(N_GROUPS))

        def t_body(t, hs):
            rsl = pl.ds(pl.multiple_of(t * nb + b0, SUBLANES), SUBLANES)
            out = []
            for j in range(N_GROUPS):
                acols, ucols = gate_cols(j)
                h = z_ref[rsl, acols] * hs[j] + z_ref[rsl, ucols]
                z_ref[rsl, ucols] = h
                out.append(h)
            return tuple(out)
        hs = lax.fori_loop(0, tt, t_body, hs, unroll=min(tt, 8))
        for j in range(N_GROUPS):
            h_ref[pl.ds(b0, SUBLANES), MXU_DIM * j:MXU_DIM * (j + 1)] = hs[j]
        return carry
    lax.fori_loop(0, nb // SUBLANES, scan_body, 0)

    for r0 in _chunks(rows):
        sl = pl.ds(r0, CHUNK)
        sg = _silu(z_ref[sl, WIDTH:2 * WIDTH])
        for j in range(N_GROUPS):
            cols = slice(MXU_DIM * j, MXU_DIM * (j + 1))
            yb_ref[sl, cols] = (z_ref[sl, gate_cols(j)[1]] * sg[:, cols]).astype(bf16)

    z_ref[:, 0:2 * WIDTH] = jnp.dot(hn_ref[...], w_in_ref[:, 6 * WIDTH:8 * WIDTH],
                                    preferred_element_type=f32)
    z_ref[:, 2 * WIDTH:3 * WIDTH] = jnp.dot(ya_ref[...], w_ao_ref[...], preferred_element_type=f32)
    z_ref[:, 3 * WIDTH:4 * WIDTH] = jnp.dot(yb_ref[...], w_bo_ref[...], preferred_element_type=f32)
    for r0 in _chunks(rows):
        sl = pl.ds(r0, CHUNK)
        m = (_sigmoid(z_ref[sl, 0:WIDTH]) * z_ref[sl, 2 * WIDTH:3 * WIDTH]
             + _sigmoid(z_ref[sl, WIDTH:2 * WIDTH]) * z_ref[sl, 3 * WIDTH:4 * WIDTH])
        ya_ref[sl, :] = m.astype(bf16)

    z_ref[:, 0:WIDTH] = jnp.dot(ya_ref[...], w_o_ref[...], preferred_element_type=f32)
    g_pe = vec(V_NORM_PE)
    for r0 in _chunks(rows):
        sl = pl.ds(r0, CHUNK)
        x1 = load_x(r0, CHUNK) + z_ref[sl, 0:WIDTH]
        store_y(r0, CHUNK, x1)
        hn_ref[sl, :] = _rms_scale(x1, g_pe).astype(bf16)

    z_ref[:, 0:WIDTH] = jnp.dot(hn_ref[...], w_pg_ref[...], preferred_element_type=f32)
    for r0 in _chunks(rows):
        ya_ref[pl.ds(r0, CHUNK), 0:PLE_DIM] = load_p(r0, CHUNK).astype(bf16)
    z_ref[:, WIDTH:2 * WIDTH] = jnp.dot(ya_ref[:, 0:PLE_DIM], w_pe_ref[...],
                                        preferred_element_type=f32)
    g_fin = vec(V_NORM_FINAL)
    for r0 in _chunks(rows):
        sl = pl.ds(r0, CHUNK)
        x2 = load_y(r0, CHUNK) + z_ref[sl, WIDTH:2 * WIDTH] * _sigmoid(z_ref[sl, 0:WIDTH])
        store_y(r0, CHUNK, _rms_scale(x2, g_fin) if final_norm else x2)

    newa_ref[...] = exta_ref[rows:rows + hist_a, :]
    newb_ref[...] = extb_ref[rows:rows + hist_b, :]
    hlast_ref[...] = h_ref[...]
    exta_ref[0:hist_a, :] = exta_ref[rows:rows + hist_a, :]
    extb_ref[0:hist_b, :] = extb_ref[rows:rows + hist_b, :]

    if batch_major_io:
        for c in out_copies(step, slot):
            c.start()

        @pl.when(step == n_steps - 1)
        def _():
            for s_back in range(min(N_SLOTS, x_in.shape[1] // tt) - 1, -1, -1):
                for c in out_copies(step - s_back, (step - s_back) % N_SLOTS):
                    c.wait()


def _run_layer(x, p, bufa0, bufb0, h0, vecs, weights, *, nb, layer, final_norm, batch_major_io):
    assert TILE_ROWS % nb == 0 and nb % SUBLANES == 0 and TILE_ROWS >= (CONV_B_WIDTH - 1) * nb
    tt = TILE_ROWS // nb
    if batch_major_io:
        assert nb == SUBLANES and x.shape[0] == nb and x.shape[1] % tt == 0
        n_steps = x.shape[1] // tt
    else:
        assert x.shape[0] % TILE_ROWS == 0
        n_steps = x.shape[0] // TILE_ROWS
    hist_a = (CONV_A_WIDTH - 1) * nb
    hist_b = (CONV_B_WIDTH - 1) * nb
    f32, bf16 = jnp.float32, jnp.bfloat16

    row_spec = lambda cols: pl.BlockSpec((TILE_ROWS, cols), lambda i: (i, 0))
    const_spec = lambda shape: pl.BlockSpec(shape, lambda i: (0,) * len(shape))
    layer_spec = lambda a: pl.BlockSpec((None,) + a.shape[1:],
                                        lambda i: (layer,) + (0,) * (a.ndim - 1),
                                        pipeline_mode=pl.Buffered(1))
    hbm = pl.BlockSpec(memory_space=pl.ANY)

    scratch = [pltpu.VMEM((TILE_ROWS, D_MODEL), bf16),
               pltpu.VMEM((TILE_ROWS, Z_COLS), f32),
               pltpu.VMEM((TILE_ROWS + hist_a, WIDTH), f32),
               pltpu.VMEM((TILE_ROWS + hist_b, WIDTH), f32),
               pltpu.VMEM((nb, WIDTH), f32),
               pltpu.VMEM((TILE_ROWS, WIDTH), bf16),
               pltpu.VMEM((TILE_ROWS, WIDTH), bf16)]
    if batch_major_io:
        scratch += [pltpu.VMEM((N_SLOTS * tt, nb, D_MODEL), f32),
                    pltpu.VMEM((N_SLOTS * tt, nb, PLE_DIM), f32),
                    pltpu.VMEM((N_SLOTS * tt, nb, D_MODEL), f32),
                    pltpu.SemaphoreType.DMA((N_SLOTS,)),
                    pltpu.SemaphoreType.DMA((N_SLOTS,)),
                    pltpu.SemaphoreType.DMA((N_SLOTS,))]
        x_spec, p_spec, y_spec = hbm, hbm, hbm
        y_shape = jax.ShapeDtypeStruct(x.shape, f32)
    else:
        x_spec, p_spec, y_spec = row_spec(D_MODEL), row_spec(PLE_DIM), row_spec(D_MODEL)
        y_shape = jax.ShapeDtypeStruct(x.shape, f32)

    return pl.pallas_call(
        functools.partial(_layer_kernel, nb=nb, layer=layer, final_norm=final_norm,
                          batch_major_io=batch_major_io),
        grid=(n_steps,),
        in_specs=[x_spec, p_spec,
                  const_spec((hist_a, WIDTH)), const_spec((hist_b, WIDTH)), const_spec((nb, WIDTH)),
                  layer_spec(vecs)] + [layer_spec(w) for w in weights],
        out_specs=[y_spec, const_spec((hist_a, WIDTH)), const_spec((hist_b, WIDTH)),
                   const_spec((nb, WIDTH))],
        out_shape=[y_shape,
                   jax.ShapeDtypeStruct((hist_a, WIDTH), f32),
                   jax.ShapeDtypeStruct((hist_b, WIDTH), f32),
                   jax.ShapeDtypeStruct((nb, WIDTH), f32)],
        scratch_shapes=scratch,
        compiler_params=pltpu.CompilerParams(dimension_semantics=("arbitrary",),
                                             vmem_limit_bytes=VMEM_LIMIT_BYTES),
        name=("prompt" if batch_major_io else "sample") + f"_layer{layer}",
    )(x, p, bufa0, bufb0, h0, vecs, *weights)


def _gate_weights(w_r, w_i):
    per = MXU_DIM // LRU_BLOCK

    def superblocks(w):
        w = w.reshape(w.shape[0], N_GROUPS, per, LRU_BLOCK, LRU_BLOCK)
        eye = jnp.eye(per, dtype=w.dtype)
        return jnp.einsum('lgaij,ab->lgaibj', w, eye).reshape(w.shape[0], N_GROUPS, MXU_DIM, MXU_DIM)
    return jnp.concatenate([superblocks(w_r), superblocks(w_i)], axis=-1)


def _time_major(a):
    b, t, c = a.shape
    return jnp.swapaxes(a, 0, 1).reshape(t * b, c)


def _batch_major(a, nb):
    return jnp.swapaxes(a.reshape(a.shape[0] // nb, nb, a.shape[1]), 0, 1)


def kernel(x_prompt, x_sample, state_conv_a, state_conv_b, state_h, p_prompt, p_sample, norm_in, w_in, conv_a_w, conv_b_w, conv_b_b, w_r, b_r, w_i, b_i, lam, w_a_out, w_b_out, w_o, norm_pe, w_pg, w_pe, norm_final):
    depth = w_in.shape[0]
    n_p, n_s = x_prompt.shape[0], x_sample.shape[0]
    f32, bf16 = jnp.float32, jnp.bfloat16

    vecs = jnp.concatenate([
        norm_in[:, None], conv_a_w, conv_b_w, conv_b_b[:, None], b_r[:, None], b_i[:, None],
        lam[:, None], norm_pe[:, None], jnp.broadcast_to(norm_final, (depth, 1, D_MODEL)),
        jnp.zeros((depth, V_ROWS - 14, WIDTH), f32)], axis=1)
    weights = (w_in.astype(bf16), _gate_weights(w_r, w_i).astype(bf16), w_a_out.astype(bf16),
               w_b_out.astype(bf16), w_o.astype(bf16), w_pg.astype(bf16), w_pe.astype(bf16))

    xp, xs = x_prompt, _time_major(x_sample)
    zeros_a = jnp.zeros(((CONV_A_WIDTH - 1) * n_p, WIDTH), f32)
    zeros_b = jnp.zeros(((CONV_B_WIDTH - 1) * n_p, WIDTH), f32)
    zeros_h = jnp.zeros((n_p, WIDTH), f32)
    outs = {k: [] for k in ("ca_p", "cb_p", "h_p", "ca_s", "cb_s", "h_s")}
    for l in range(depth):
        final = l == depth - 1
        xp, na, nb_, nh = _run_layer(xp, p_prompt, zeros_a, zeros_b, zeros_h, vecs, weights,
                                     nb=n_p, layer=l, final_norm=final, batch_major_io=True)
        outs["ca_p"].append(_batch_major(na, n_p))
        outs["cb_p"].append(_batch_major(nb_, n_p))
        outs["h_p"].append(nh)
        xs, na, nb_, nh = _run_layer(xs, _time_major(p_sample[l]), _time_major(state_conv_a[l]),
                                     _time_major(state_conv_b[l]), state_h[l], vecs, weights,
                                     nb=n_s, layer=l, final_norm=final, batch_major_io=False)
        outs["ca_s"].append(_batch_major(na, n_s))
        outs["cb_s"].append(_batch_major(nb_, n_s))
        outs["h_s"].append(nh)
    return (xp, _batch_major(xs, n_s),
            jnp.stack(outs["ca_p"]), jnp.stack(outs["cb_p"]), jnp.stack(outs["h_p"]),
            jnp.stack(outs["ca_s"]), jnp.stack(outs["cb_s"]), jnp.stack(outs["h_s"]))
```

```python
import functools

import jax
import jax.numpy as jnp
from jax import lax
from jax.experimental import pallas as pl
from jax.experimental.pallas import tpu as pltpu

D_MODEL = 1024
WIDTH = 1024
PLE_DIM = 256
LRU_HEADS = 16
LRU_BLOCK = WIDTH // LRU_HEADS
LRU_C = 8.0
RMS_EPS = 1e-6
CONV_A_WIDTH = 3
CONV_B_WIDTH = 4

MXU_DIM = 256
N_GROUPS = WIDTH // MXU_DIM
SUBLANES = 8
TILE_ROWS = 512
CHUNK = 16
Z_COLS = 6 * WIDTH
N_PARTS = 2
PART_ROWS = TILE_ROWS // N_PARTS
N_SLOTS = 2

V_NORM_IN, V_CONV_A, V_CONV_B, V_CONV_B_BIAS = 0, 1, 4, 8
V_B_R, V_B_I, V_LAM, V_NORM_PE, V_NORM_FINAL = 9, 10, 11, 12, 13
V_ROWS = 16

VMEM_LIMIT_BYTES = 62 * 1024 * 1024


def _sigmoid(x):
    return 0.5 * jnp.tanh(0.5 * x) + 0.5


def _silu(x):
    h = 0.5 * x
    return h + h * jnp.tanh(h)


def _softplus(y):
    return jnp.maximum(y, 0.0) + jnp.log1p(jnp.exp(-jnp.abs(y)))


def _rms_scale(x, g):
    ms = jnp.mean(x * x, axis=-1, keepdims=True)
    return x * lax.rsqrt(ms + RMS_EPS) * g


def _chunks(n_rows):
    return range(0, n_rows, CHUNK)


def _layer_kernel(x_in, p_in, bufa0_ref, bufb0_ref, h0_ref, vec_ref,
                  w_in_ref, w_ri_ref, w_ao_ref, w_bo_ref, w_o_ref, w_pg_ref, w_pe_ref,
                  y_out, newa_ref, newb_ref, hlast_ref,
                  hn_ref, z_ref, exta_ref, extb_ref, h_ref, ya_ref, yb_ref, pb_ref, *io_scratch,
                  nb, layer, final_norm, batch_major_io):
    rows = TILE_ROWS
    tt = rows // nb
    hist_a = (CONV_A_WIDTH - 1) * nb
    hist_b = (CONV_B_WIDTH - 1) * nb
    f32, bf16 = jnp.float32, jnp.bfloat16
    step = pl.program_id(0)
    n_steps = pl.num_programs(0)

    if batch_major_io:
        xbuf, pbuf, ybuf, sem_x, sem_p, sem_y = io_scratch
        slot = step % N_SLOTS

        def in_copies(s, sl):
            t0 = s * tt
            dst = pl.ds(sl * tt, tt)
            cps = []
            for b in range(nb):
                cps.append(pltpu.make_async_copy(x_in.at[b, pl.ds(t0, tt), :],
                                                 xbuf.at[dst, b, :], sem_x.at[sl]))
                cps.append(pltpu.make_async_copy(p_in.at[layer, b, pl.ds(t0, tt), :],
                                                 pbuf.at[dst, b, :], sem_p.at[sl]))
            return cps

        def out_copies(s, sl):
            src = pl.ds(sl * tt, tt)
            return [pltpu.make_async_copy(ybuf.at[src, b, :], y_out.at[b, pl.ds(s * tt, tt), :],
                                          sem_y.at[sl]) for b in range(nb)]

        @pl.when(step == 0)
        def _():
            for c in in_copies(0, 0):
                c.start()

        @pl.when(step + 1 < n_steps)
        def _():
            for c in in_copies(step + 1, (step + 1) % N_SLOTS):
                c.start()

        for c in in_copies(step, slot):
            c.wait()

        @pl.when(step >= N_SLOTS)
        def _():
            for c in out_copies(step - N_SLOTS, slot):
                c.wait()

        def staged(r0, n):
            return pl.ds(slot * tt + r0 // nb, n // nb)

        def load_x(r0, n):
            return xbuf[staged(r0, n), :, :].reshape(n, D_MODEL)

        def load_p(r0, n):
            return pbuf[staged(r0, n), :, :].reshape(n, PLE_DIM)

        def load_y(r0, n):
            return ybuf[staged(r0, n), :, :].reshape(n, D_MODEL)

        def store_y(r0, n, val):
            ybuf[staged(r0, n), :, :] = val.reshape(n // nb, nb, D_MODEL)
    else:
        def load_x(r0, n):
            return x_in[pl.ds(r0, n), :]

        def load_p(r0, n):
            return p_in[pl.ds(r0, n), :]

        def load_y(r0, n):
            return y_out[pl.ds(r0, n), :]

        def store_y(r0, n, val):
            y_out[pl.ds(r0, n), :] = val

    @pl.when(step == 0)
    def _():
        exta_ref[0:hist_a, :] = bufa0_ref[...]
        extb_ref[0:hist_b, :] = bufb0_ref[...]
        h_ref[...] = h0_ref[...]

    def vec(row, n=1):
        return vec_ref[row:row + n, :]

    g_in, g_pe, g_fin = vec(V_NORM_IN), vec(V_NORM_PE), vec(V_NORM_FINAL)
    wa = [vec(V_CONV_A + k) for k in range(CONV_A_WIDTH)]
    wb = [vec(V_CONV_B + k) for k in range(CONV_B_WIDTH)]
    bias_b = vec(V_CONV_B_BIAS)
    half_neg_c_sp = (-0.5 * LRU_C) * _softplus(-vec(V_LAM))
    half_b_r, half_b_i = 0.5 * vec(V_B_R), 0.5 * vec(V_B_I)
    xcb_ref = yb_ref

    COL_B = 0 * WIDTH
    COL_A = 2 * WIDTH
    COL_MERGE = COL_A
    COL_GATE = COL_A + 2 * WIDTH

    def gate_cols(j):
        c0 = COL_GATE + 2 * MXU_DIM * j
        return slice(c0, c0 + MXU_DIM), slice(c0 + MXU_DIM, c0 + 2 * MXU_DIM)

    def rows_of(part):
        return pl.ds(part * PART_ROWS, PART_ROWS)

    def dot(a, b):
        return jnp.dot(a, b, preferred_element_type=f32)

    def ew_norm_in(part):
        g0 = part * PART_ROWS
        for r0 in _chunks(PART_ROWS):
            hn_ref[pl.ds(g0 + r0, CHUNK), :] = _rms_scale(load_x(g0 + r0, CHUNK), g_in).astype(bf16)

    def mm_branch_b(part):
        z_ref[part, :, COL_B:COL_B + 2 * WIDTH] = dot(hn_ref[rows_of(part), :],
                                                      w_in_ref[:, 4 * WIDTH:6 * WIDTH])

    def mm_branch_a(part):
        z_ref[part, :, COL_A:COL_A + 4 * WIDTH] = dot(hn_ref[rows_of(part), :],
                                                      w_in_ref[:, 0:4 * WIDTH])

    def ew_conv_b(part):
        g0, z = part * PART_ROWS, z_ref.at[part]
        for r0 in _chunks(PART_ROWS):
            sl = pl.ds(r0, CHUNK)
            xb = z[sl, COL_B:COL_B + WIDTH]
            extb_ref[pl.ds(g0 + r0 + hist_b, CHUNK), :] = xb
            xc = xb * wb[CONV_B_WIDTH - 1] + bias_b
            for k in range(CONV_B_WIDTH - 1):
                xc = xc + extb_ref[pl.ds(g0 + r0 + k * nb, CHUNK), :] * wb[k]
            z[sl, COL_B:COL_B + WIDTH] = xc
            xcb_ref[pl.ds(g0 + r0, CHUNK), :] = xc.astype(bf16)

    def ew_branch_a(part):
        g0, z = part * PART_ROWS, z_ref.at[part]
        for r0 in _chunks(PART_ROWS):
            sl = pl.ds(r0, CHUNK)
            v = z[sl, COL_A + 2 * WIDTH:COL_A + 3 * WIDTH] * z[sl, COL_A:COL_A + WIDTH]
            exta_ref[pl.ds(g0 + r0 + hist_a, CHUNK), :] = v
            conv = v * wa[CONV_A_WIDTH - 1]
            for k in range(CONV_A_WIDTH - 1):
                conv = conv + exta_ref[pl.ds(g0 + r0 + k * nb, CHUNK), :] * wa[k]
            ya = (z[sl, COL_A + WIDTH:COL_A + 2 * WIDTH] * conv
                  * _silu(z[sl, COL_A + 3 * WIDTH:COL_A + 4 * WIDTH]))
            ya_ref[pl.ds(g0 + r0, CHUNK), :] = ya.astype(bf16)

    def mm_gates(part):
        for j in range(N_GROUPS):
            z_ref[part, :, COL_GATE + 2 * MXU_DIM * j:COL_GATE + 2 * MXU_DIM * (j + 1)] = dot(
                xcb_ref[rows_of(part), MXU_DIM * j:MXU_DIM * (j + 1)], w_ri_ref[j])

    def mm_merge_gates(part):
        z_ref[part, :, COL_MERGE:COL_MERGE + 2 * WIDTH] = dot(hn_ref[rows_of(part), :],
                                                              w_in_ref[:, 6 * WIDTH:8 * WIDTH])

    def ew_gates(part):
        z = z_ref.at[part]
        for r0 in _chunks(PART_ROWS):
            sl = pl.ds(r0, CHUNK)
            for j in range(N_GROUPS):
                cols = slice(MXU_DIM * j, MXU_DIM * (j + 1))
                rcols, icols = gate_cols(j)
                t_r = jnp.tanh(0.5 * z[sl, rcols] + half_b_r[:, cols])
                gi = 0.5 * jnp.tanh(0.5 * z[sl, icols] + half_b_i[:, cols]) + 0.5
                log_a = t_r * half_neg_c_sp[:, cols] + half_neg_c_sp[:, cols]
                a = jnp.exp(log_a)
                s = jnp.tanh(log_a) * (-1.0 - a * a)
                beta = jnp.where(s > 0.0, s * lax.rsqrt(s), 0.0)
                z[sl, rcols] = a
                z[sl, icols] = beta * gi * z[sl, COL_B + MXU_DIM * j:COL_B + MXU_DIM * (j + 1)]

    def ew_scan(part):
        z = z_ref.at[part]
        for b0 in range(0, nb, SUBLANES):
            for j in range(N_GROUPS):
                acols, ucols = gate_cols(j)
                hcols = slice(MXU_DIM * j, MXU_DIM * (j + 1))
                h = h_ref[b0:b0 + SUBLANES, hcols]
                for t in range(PART_ROWS // nb):
                    rsl = pl.ds(t * nb + b0, SUBLANES)
                    h = z[rsl, acols] * h + z[rsl, ucols]
                    z[rsl, ucols] = h
                h_ref[b0:b0 + SUBLANES, hcols] = h

    def mm_ya_out(part):
        z_ref[part, :, COL_B:COL_B + WIDTH] = dot(ya_ref[rows_of(part), :], w_ao_ref[...])

    def ew_yb(part):
        g0, z = part * PART_ROWS, z_ref.at[part]
        for r0 in _chunks(PART_ROWS):
            sl = pl.ds(r0, CHUNK)
            sg = _silu(z[sl, COL_B + WIDTH:COL_B + 2 * WIDTH])
            for j in range(N_GROUPS):
                cols = slice(MXU_DIM * j, MXU_DIM * (j + 1))
                yb_ref[pl.ds(g0 + r0, CHUNK), cols] = (z[sl, gate_cols(j)[1]] * sg[:, cols]).astype(bf16)

    def ew_p_cast(part):
        g0 = part * PART_ROWS
        for r0 in _chunks(PART_ROWS):
            pb_ref[pl.ds(g0 + r0, CHUNK), :] = load_p(g0 + r0, CHUNK).astype(bf16)

    def mm_embed(part):
        z_ref[part, :, COL_GATE:COL_GATE + WIDTH] = dot(pb_ref[rows_of(part), :], w_pe_ref[...])

    def mm_yb_out(part):
        z_ref[part, :, COL_B + WIDTH:COL_B + 2 * WIDTH] = dot(yb_ref[rows_of(part), :], w_bo_ref[...])

    def ew_merge(part):
        g0, z = part * PART_ROWS, z_ref.at[part]
        for r0 in _chunks(PART_ROWS):
            sl = pl.ds(r0, CHUNK)
            m = 0.5 * ((jnp.tanh(0.5 * z[sl, COL_MERGE:COL_MERGE + WIDTH]) + 1.0)
                       * z[sl, COL_B:COL_B + WIDTH]
                       + (jnp.tanh(0.5 * z[sl, COL_MERGE + WIDTH:COL_MERGE + 2 * WIDTH]) + 1.0)
                       * z[sl, COL_B + WIDTH:COL_B + 2 * WIDTH])
            ya_ref[pl.ds(g0 + r0, CHUNK), :] = m.astype(bf16)

    def mm_out(part):
        z_ref[part, :, COL_MERGE:COL_MERGE + WIDTH] = dot(ya_ref[rows_of(part), :], w_o_ref[...])

    def ew_resid(part):
        g0, z = part * PART_ROWS, z_ref.at[part]
        for r0 in _chunks(PART_ROWS):
            x1 = load_x(g0 + r0, CHUNK) + z[pl.ds(r0, CHUNK), COL_MERGE:COL_MERGE + WIDTH]
            store_y(g0 + r0, CHUNK, x1)
            hn_ref[pl.ds(g0 + r0, CHUNK), :] = _rms_scale(x1, g_pe).astype(bf16)

    def mm_embed_gate(part):
        z_ref[part, :, COL_MERGE + WIDTH:COL_MERGE + 2 * WIDTH] = dot(hn_ref[rows_of(part), :],
                                                                      w_pg_ref[...])

    def ew_embed(part):
        g0, z = part * PART_ROWS, z_ref.at[part]
        for r0 in _chunks(PART_ROWS):
            sl = pl.ds(r0, CHUNK)
            x2 = load_y(g0 + r0, CHUNK) + (z[sl, COL_GATE:COL_GATE + WIDTH]
                                          * _sigmoid(z[sl, COL_MERGE + WIDTH:COL_MERGE + 2 * WIDTH]))
            store_y(g0 + r0, CHUNK, _rms_scale(x2, g_fin) if final_norm else x2)

    phases = (ew_norm_in, mm_branch_b, ew_conv_b, mm_branch_a, ew_branch_a, mm_gates, ew_gates,
              mm_merge_gates, ew_scan, mm_ya_out, ew_yb, ew_p_cast, mm_embed, mm_yb_out, ew_merge,
              mm_out, ew_resid, mm_embed_gate, ew_embed)
    for phase in phases:
        for part in range(N_PARTS):
            phase(part)

    newa_ref[...] = exta_ref[rows:rows + hist_a, :]
    newb_ref[...] = extb_ref[rows:rows + hist_b, :]
    hlast_ref[...] = h_ref[...]
    exta_ref[0:hist_a, :] = exta_ref[rows:rows + hist_a, :]
    extb_ref[0:hist_b, :] = extb_ref[rows:rows + hist_b, :]

    if batch_major_io:
        for c in out_copies(step, slot):
            c.start()

        @pl.when(step == n_steps - 1)
        def _():
            for s_back in range(min(N_SLOTS, x_in.shape[1] // tt) - 1, -1, -1):
                for c in out_copies(step - s_back, (step - s_back) % N_SLOTS):
                    c.wait()


def _run_layer(x, p, bufa0, bufb0, h0, vecs, weights, *, nb, layer, final_norm, batch_major_io):
    assert TILE_ROWS % nb == 0 and nb % SUBLANES == 0 and TILE_ROWS >= (CONV_B_WIDTH - 1) * nb
    tt = TILE_ROWS // nb
    if batch_major_io:
        assert nb == SUBLANES and x.shape[0] == nb and x.shape[1] % tt == 0
        n_steps = x.shape[1] // tt
    else:
        assert x.shape[0] % TILE_ROWS == 0
        n_steps = x.shape[0] // TILE_ROWS
    hist_a = (CONV_A_WIDTH - 1) * nb
    hist_b = (CONV_B_WIDTH - 1) * nb
    f32, bf16 = jnp.float32, jnp.bfloat16

    row_spec = lambda cols: pl.BlockSpec((TILE_ROWS, cols), lambda i: (i, 0))
    const_spec = lambda shape: pl.BlockSpec(shape, lambda i: (0,) * len(shape))
    layer_spec = lambda a: pl.BlockSpec((None,) + a.shape[1:],
                                        lambda i: (layer,) + (0,) * (a.ndim - 1),
                                        pipeline_mode=pl.Buffered(1))
    hbm = pl.BlockSpec(memory_space=pl.ANY)

    scratch = [pltpu.VMEM((TILE_ROWS, D_MODEL), bf16),
               pltpu.VMEM((N_PARTS, PART_ROWS, Z_COLS), f32),
               pltpu.VMEM((TILE_ROWS + hist_a, WIDTH), f32),
               pltpu.VMEM((TILE_ROWS + hist_b, WIDTH), f32),
               pltpu.VMEM((nb, WIDTH), f32),
               pltpu.VMEM((TILE_ROWS, WIDTH), bf16),
               pltpu.VMEM((TILE_ROWS, WIDTH), bf16),
               pltpu.VMEM((TILE_ROWS, PLE_DIM), bf16)]
    if batch_major_io:
        scratch += [pltpu.VMEM((N_SLOTS * tt, nb, D_MODEL), f32),
                    pltpu.VMEM((N_SLOTS * tt, nb, PLE_DIM), f32),
                    pltpu.VMEM((N_SLOTS * tt, nb, D_MODEL), f32),
                    pltpu.SemaphoreType.DMA((N_SLOTS,)),
                    pltpu.SemaphoreType.DMA((N_SLOTS,)),
                    pltpu.SemaphoreType.DMA((N_SLOTS,))]
        x_spec, p_spec, y_spec = hbm, hbm, hbm
        y_shape = jax.ShapeDtypeStruct(x.shape, f32)
    else:
        x_spec, p_spec, y_spec = row_spec(D_MODEL), row_spec(PLE_DIM), row_spec(D_MODEL)
        y_shape = jax.ShapeDtypeStruct(x.shape, f32)

    return pl.pallas_call(
        functools.partial(_layer_kernel, nb=nb, layer=layer, final_norm=final_norm,
                          batch_major_io=batch_major_io),
        grid=(n_steps,),
        in_specs=[x_spec, p_spec,
                  const_spec((hist_a, WIDTH)), const_spec((hist_b, WIDTH)), const_spec((nb, WIDTH)),
                  layer_spec(vecs)] + [layer_spec(w) for w in weights],
        out_specs=[y_spec, const_spec((hist_a, WIDTH)), const_spec((hist_b, WIDTH)),
                   const_spec((nb, WIDTH))],
        out_shape=[y_shape,
                   jax.ShapeDtypeStruct((hist_a, WIDTH), f32),
                   jax.ShapeDtypeStruct((hist_b, WIDTH), f32),
                   jax.ShapeDtypeStruct((nb, WIDTH), f32)],
        scratch_shapes=scratch,
        compiler_params=pltpu.CompilerParams(dimension_semantics=("arbitrary",),
                                             vmem_limit_bytes=VMEM_LIMIT_BYTES),
        name=("prompt" if batch_major_io else "sample") + f"_layer{layer}",
    )(x, p, bufa0, bufb0, h0, vecs, *weights)


def _gate_weights(w_r, w_i):
    per = MXU_DIM // LRU_BLOCK

    def superblocks(w):
        w = w.reshape(w.shape[0], N_GROUPS, per, LRU_BLOCK, LRU_BLOCK)
        eye = jnp.eye(per, dtype=w.dtype)
        return jnp.einsum('lgaij,ab->lgaibj', w, eye).reshape(w.shape[0], N_GROUPS, MXU_DIM, MXU_DIM)
    return jnp.concatenate([superblocks(w_r), superblocks(w_i)], axis=-1)


def _time_major(a):
    b, t, c = a.shape
    return jnp.swapaxes(a, 0, 1).reshape(t * b, c)


def _batch_major(a, nb):
    return jnp.swapaxes(a.reshape(a.shape[0] // nb, nb, a.shape[1]), 0, 1)


def kernel(x_prompt, x_sample, state_conv_a, state_conv_b, state_h, p_prompt, p_sample, norm_in, w_in, conv_a_w, conv_b_w, conv_b_b, w_r, b_r, w_i, b_i, lam, w_a_out, w_b_out, w_o, norm_pe, w_pg, w_pe, norm_final):
    depth = w_in.shape[0]
    n_p, n_s = x_prompt.shape[0], x_sample.shape[0]
    f32, bf16 = jnp.float32, jnp.bfloat16

    vecs = jnp.concatenate([
        norm_in[:, None], conv_a_w, conv_b_w, conv_b_b[:, None], b_r[:, None], b_i[:, None],
        lam[:, None], norm_pe[:, None], jnp.broadcast_to(norm_final, (depth, 1, D_MODEL)),
        jnp.zeros((depth, V_ROWS - 14, WIDTH), f32)], axis=1)
    weights = (w_in.astype(bf16), _gate_weights(w_r, w_i).astype(bf16), w_a_out.astype(bf16),
               w_b_out.astype(bf16), w_o.astype(bf16), w_pg.astype(bf16), w_pe.astype(bf16))

    xp, xs = x_prompt, _time_major(x_sample)
    zeros_a = jnp.zeros(((CONV_A_WIDTH - 1) * n_p, WIDTH), f32)
    zeros_b = jnp.zeros(((CONV_B_WIDTH - 1) * n_p, WIDTH), f32)
    zeros_h = jnp.zeros((n_p, WIDTH), f32)
    outs = {k: [] for k in ("ca_p", "cb_p", "h_p", "ca_s", "cb_s", "h_s")}
    for l in range(depth):
        final = l == depth - 1
        xp, na, nb_, nh = _run_layer(xp, p_prompt, zeros_a, zeros_b, zeros_h, vecs, weights,
                                     nb=n_p, layer=l, final_norm=final, batch_major_io=True)
        outs["ca_p"].append(_batch_major(na, n_p))
        outs["cb_p"].append(_batch_major(nb_, n_p))
        outs["h_p"].append(nh)
        xs, na, nb_, nh = _run_layer(xs, _time_major(p_sample[l]), _time_major(state_conv_a[l]),
                                     _time_major(state_conv_b[l]), state_h[l], vecs, weights,
                                     nb=n_s, layer=l, final_norm=final, batch_major_io=False)
        outs["ca_s"].append(_batch_major(na, n_s))
        outs["cb_s"].append(_batch_major(nb_, n_s))
        outs["h_s"].append(nh)
    return (xp, _batch_major(xs, n_s),
            jnp.stack(outs["ca_p"]), jnp.stack(outs["cb_p"]), jnp.stack(outs["h_p"]),
            jnp.stack(outs["ca_s"]), jnp.stack(outs["cb_s"]), jnp.stack(outs["h_s"]))
```

```python
import functools

import jax
import jax.numpy as jnp
from jax import lax
from jax.experimental import pallas as pl
from jax.experimental.pallas import tpu as pltpu

D_MODEL = 1024
WIDTH = 1024
PLE_DIM = 256
LRU_HEADS = 16
LRU_BLOCK = WIDTH // LRU_HEADS
LRU_C = 8.0
RMS_EPS = 1e-6
CONV_A_WIDTH = 3
CONV_B_WIDTH = 4

MXU_DIM = 256
N_GROUPS = WIDTH // MXU_DIM
SUBLANES = 8
TILE_ROWS = 512
CHUNK = 16
Z_COLS = 6 * WIDTH
N_PARTS = 2
PART_ROWS = TILE_ROWS // N_PARTS
N_SLOTS = 2

V_NORM_IN, V_CONV_A, V_CONV_B, V_CONV_B_BIAS = 0, 1, 4, 8
V_B_R, V_B_I, V_LAM, V_NORM_PE, V_NORM_FINAL = 9, 10, 11, 12, 13
V_ROWS = 16

VMEM_LIMIT_BYTES = 62 * 1024 * 1024


def _two_sigmoid_of_twice(h):
    return jnp.tanh(h) + 1.0


def _silu_of_twice(h):
    return h + h * jnp.tanh(h)


def _softplus(y):
    return jnp.maximum(y, 0.0) + jnp.log1p(jnp.exp(-jnp.abs(y)))


def _rms_scale(x, g):
    ms = jnp.mean(x * x, axis=-1, keepdims=True)
    return x * lax.rsqrt(ms + RMS_EPS) * g


def _chunks(n_rows):
    return range(0, n_rows, CHUNK)


def _layer_kernel(x_in, p_in, bufa0_ref, bufb0_ref, h0_ref, vec_ref,
                  w_in_ref, w_ri_ref, w_ao_ref, w_bo_ref, w_o_ref, w_pg_ref, w_pe_ref,
                  y_out, newa_ref, newb_ref, hlast_ref,
                  hn_ref, z_ref, exta_ref, extb_ref, h_ref, ya_ref, yb_ref, pb_ref, *io_scratch,
                  nb, layer, final_norm, batch_major_io):
    rows = TILE_ROWS
    tt = rows // nb
    hist_a = (CONV_A_WIDTH - 1) * nb
    hist_b = (CONV_B_WIDTH - 1) * nb
    f32, bf16 = jnp.float32, jnp.bfloat16
    step = pl.program_id(0)
    n_steps = pl.num_programs(0)

    if batch_major_io:
        xbuf, pbuf, ybuf, sem_x, sem_p, sem_y = io_scratch
        slot = step % N_SLOTS

        def in_copies(s, sl):
            t0 = s * tt
            dst = pl.ds(sl * tt, tt)
            cps = []
            for b in range(nb):
                cps.append(pltpu.make_async_copy(x_in.at[b, pl.ds(t0, tt), :],
                                                 xbuf.at[dst, b, :], sem_x.at[sl]))
                cps.append(pltpu.make_async_copy(p_in.at[layer, b, pl.ds(t0, tt), :],
                                                 pbuf.at[dst, b, :], sem_p.at[sl]))
            return cps

        def out_copies(s, sl):
            src = pl.ds(sl * tt, tt)
            return [pltpu.make_async_copy(ybuf.at[src, b, :], y_out.at[b, pl.ds(s * tt, tt), :],
                                          sem_y.at[sl]) for b in range(nb)]

        @pl.when(step == 0)
        def _():
            for c in in_copies(0, 0):
                c.start()

        @pl.when(step + 1 < n_steps)
        def _():
            for c in in_copies(step + 1, (step + 1) % N_SLOTS):
                c.start()

        for c in in_copies(step, slot):
            c.wait()

        @pl.when(step >= N_SLOTS)
        def _():
            for c in out_copies(step - N_SLOTS, slot):
                c.wait()

        def staged(r0, n):
            return pl.ds(slot * tt + r0 // nb, n // nb)

        def load_x(r0, n):
            return xbuf[staged(r0, n), :, :].reshape(n, D_MODEL)

        def load_p(r0, n):
            return pbuf[staged(r0, n), :, :].reshape(n, PLE_DIM)

        def load_y(r0, n):
            return ybuf[staged(r0, n), :, :].reshape(n, D_MODEL)

        def store_y(r0, n, val):
            ybuf[staged(r0, n), :, :] = val.reshape(n // nb, nb, D_MODEL)
    else:
        def load_x(r0, n):
            return x_in[pl.ds(r0, n), :]

        def load_p(r0, n):
            return p_in[pl.ds(r0, n), :]

        def load_y(r0, n):
            return y_out[pl.ds(r0, n), :]

        def store_y(r0, n, val):
            y_out[pl.ds(r0, n), :] = val

    @pl.when(step == 0)
    def _():
        exta_ref[0:hist_a, :] = bufa0_ref[...]
        extb_ref[0:hist_b, :] = bufb0_ref[...]
        h_ref[...] = h0_ref[...]

    def vec(row, n=1):
        return vec_ref[row:row + n, :]

    g_in, g_pe, g_fin = vec(V_NORM_IN), vec(V_NORM_PE), vec(V_NORM_FINAL)
    wa = [vec(V_CONV_A + k) for k in range(CONV_A_WIDTH)]
    wb = [0.5 * vec(V_CONV_B + k) for k in range(CONV_B_WIDTH)]
    bias_b = 0.5 * vec(V_CONV_B_BIAS)
    half_neg_c_sp = (-0.5 * LRU_C) * _softplus(-vec(V_LAM))
    half_b_r, half_b_i = 0.5 * vec(V_B_R), 0.5 * vec(V_B_I)
    xcb_ref = yb_ref

    COL_B = 0 * WIDTH
    COL_A = 2 * WIDTH
    COL_MERGE = COL_A
    COL_GATE = COL_A + 2 * WIDTH

    def gate_cols(j):
        c0 = COL_GATE + 2 * MXU_DIM * j
        return slice(c0, c0 + MXU_DIM), slice(c0 + MXU_DIM, c0 + 2 * MXU_DIM)

    def rows_of(part):
        return pl.ds(part * PART_ROWS, PART_ROWS)

    def dot(a, b):
        return jnp.dot(a, b, preferred_element_type=f32)

    def ew_norm_in(part):
        g0 = part * PART_ROWS
        for r0 in _chunks(PART_ROWS):
            hn_ref[pl.ds(g0 + r0, CHUNK), :] = _rms_scale(load_x(g0 + r0, CHUNK), g_in).astype(bf16)

    def mm_branch_b(part):
        z_ref[part, :, COL_B:COL_B + 2 * WIDTH] = dot(hn_ref[rows_of(part), :],
                                                      w_in_ref[:, 4 * WIDTH:6 * WIDTH])

    def mm_branch_a(part):
        z_ref[part, :, COL_A:COL_A + 4 * WIDTH] = dot(hn_ref[rows_of(part), :],
                                                      w_in_ref[:, 0:4 * WIDTH])

    def ew_conv_b(part):
        g0, z = part * PART_ROWS, z_ref.at[part]
        for r0 in _chunks(PART_ROWS):
            sl = pl.ds(r0, CHUNK)
            xb = z[sl, COL_B:COL_B + WIDTH]
            extb_ref[pl.ds(g0 + r0 + hist_b, CHUNK), :] = xb
            xh = xb * wb[CONV_B_WIDTH - 1] + bias_b
            for k in range(CONV_B_WIDTH - 1):
                xh = xh + extb_ref[pl.ds(g0 + r0 + k * nb, CHUNK), :] * wb[k]
            z[sl, COL_B:COL_B + WIDTH] = xh
            xcb_ref[pl.ds(g0 + r0, CHUNK), :] = xh.astype(bf16)

    def ew_branch_a(part):
        g0, z = part * PART_ROWS, z_ref.at[part]
        for r0 in _chunks(PART_ROWS):
            sl = pl.ds(r0, CHUNK)
            v = z[sl, COL_A + 2 * WIDTH:COL_A + 3 * WIDTH] * z[sl, COL_A:COL_A + WIDTH]
            exta_ref[pl.ds(g0 + r0 + hist_a, CHUNK), :] = v
            conv = v * wa[CONV_A_WIDTH - 1]
            for k in range(CONV_A_WIDTH - 1):
                conv = conv + exta_ref[pl.ds(g0 + r0 + k * nb, CHUNK), :] * wa[k]
            ya = (z[sl, COL_A + WIDTH:COL_A + 2 * WIDTH] * conv
                  * _silu_of_twice(z[sl, COL_A + 3 * WIDTH:COL_A + 4 * WIDTH]))
            ya_ref[pl.ds(g0 + r0, CHUNK), :] = ya.astype(bf16)

    def mm_gates(part):
        for j in range(N_GROUPS):
            z_ref[part, :, COL_GATE + 2 * MXU_DIM * j:COL_GATE + 2 * MXU_DIM * (j + 1)] = dot(
                xcb_ref[rows_of(part), MXU_DIM * j:MXU_DIM * (j + 1)], w_ri_ref[j])

    def mm_merge_gates(part):
        z_ref[part, :, COL_MERGE:COL_MERGE + 2 * WIDTH] = dot(hn_ref[rows_of(part), :],
                                                              w_in_ref[:, 6 * WIDTH:8 * WIDTH])

    def ew_gates(part):
        z = z_ref.at[part]
        for r0 in _chunks(PART_ROWS):
            sl = pl.ds(r0, CHUNK)
            for j in range(N_GROUPS):
                cols = slice(MXU_DIM * j, MXU_DIM * (j + 1))
                rcols, icols = gate_cols(j)
                t_r = jnp.tanh(z[sl, rcols] + half_b_r[:, cols])
                gi2 = _two_sigmoid_of_twice(z[sl, icols] + half_b_i[:, cols])
                log_a = t_r * half_neg_c_sp[:, cols] + half_neg_c_sp[:, cols]
                a = jnp.exp(log_a)
                s = jnp.tanh(log_a) * (-1.0 - a * a)
                beta = jnp.where(s > 0.0, s * lax.rsqrt(s), 0.0)
                z[sl, rcols] = a
                z[sl, icols] = beta * gi2 * z[sl, COL_B + MXU_DIM * j:COL_B + MXU_DIM * (j + 1)]

    def ew_scan(part):
        z = z_ref.at[part]
        for b0 in range(0, nb, SUBLANES):
            for j in range(N_GROUPS):
                acols, ucols = gate_cols(j)
                hcols = slice(MXU_DIM * j, MXU_DIM * (j + 1))
                h = h_ref[b0:b0 + SUBLANES, hcols]
                for t in range(PART_ROWS // nb):
                    rsl = pl.ds(t * nb + b0, SUBLANES)
                    h = z[rsl, acols] * h + z[rsl, ucols]
                    z[rsl, ucols] = h
                h_ref[b0:b0 + SUBLANES, hcols] = h

    def mm_ya_out(part):
        z_ref[part, :, COL_B:COL_B + WIDTH] = dot(ya_ref[rows_of(part), :], w_ao_ref[...])

    def ew_yb(part):
        g0, z = part * PART_ROWS, z_ref.at[part]
        for r0 in _chunks(PART_ROWS):
            sl = pl.ds(r0, CHUNK)
            sg = _silu_of_twice(z[sl, COL_B + WIDTH:COL_B + 2 * WIDTH])
            for j in range(N_GROUPS):
                cols = slice(MXU_DIM * j, MXU_DIM * (j + 1))
                yb_ref[pl.ds(g0 + r0, CHUNK), cols] = (z[sl, gate_cols(j)[1]] * sg[:, cols]).astype(bf16)

    def ew_p_cast(part):
        g0 = part * PART_ROWS
        for r0 in _chunks(PART_ROWS):
            pb_ref[pl.ds(g0 + r0, CHUNK), :] = load_p(g0 + r0, CHUNK).astype(bf16)

    def mm_embed(part):
        z_ref[part, :, COL_GATE:COL_GATE + WIDTH] = dot(pb_ref[rows_of(part), :], w_pe_ref[...])

    def mm_yb_out(part):
        z_ref[part, :, COL_B + WIDTH:COL_B + 2 * WIDTH] = dot(yb_ref[rows_of(part), :], w_bo_ref[...])

    def ew_merge(part):
        g0, z = part * PART_ROWS, z_ref.at[part]
        for r0 in _chunks(PART_ROWS):
            sl = pl.ds(r0, CHUNK)
            m = (_two_sigmoid_of_twice(z[sl, COL_MERGE:COL_MERGE + WIDTH]) * z[sl, COL_B:COL_B + WIDTH]
                 + _two_sigmoid_of_twice(z[sl, COL_MERGE + WIDTH:COL_MERGE + 2 * WIDTH])
                 * z[sl, COL_B + WIDTH:COL_B + 2 * WIDTH])
            ya_ref[pl.ds(g0 + r0, CHUNK), :] = m.astype(bf16)

    def mm_out(part):
        z_ref[part, :, COL_MERGE:COL_MERGE + WIDTH] = dot(ya_ref[rows_of(part), :], w_o_ref[...])

    def ew_resid(part):
        g0, z = part * PART_ROWS, z_ref.at[part]
        for r0 in _chunks(PART_ROWS):
            x1 = load_x(g0 + r0, CHUNK) + z[pl.ds(r0, CHUNK), COL_MERGE:COL_MERGE + WIDTH]
            store_y(g0 + r0, CHUNK, x1)
            hn_ref[pl.ds(g0 + r0, CHUNK), :] = _rms_scale(x1, g_pe).astype(bf16)

    def mm_embed_gate(part):
        z_ref[part, :, COL_MERGE + WIDTH:COL_MERGE + 2 * WIDTH] = dot(hn_ref[rows_of(part), :],
                                                                      w_pg_ref[...])

    def ew_embed(part):
        g0, z = part * PART_ROWS, z_ref.at[part]
        for r0 in _chunks(PART_ROWS):
            sl = pl.ds(r0, CHUNK)
            x2 = load_y(g0 + r0, CHUNK) + (
                z[sl, COL_GATE:COL_GATE + WIDTH]
                * _two_sigmoid_of_twice(z[sl, COL_MERGE + WIDTH:COL_MERGE + 2 * WIDTH]))
            store_y(g0 + r0, CHUNK, _rms_scale(x2, g_fin) if final_norm else x2)

    assert N_PARTS == 2
    order = ((ew_norm_in, 0), (mm_branch_b, 0), (ew_norm_in, 1), (ew_conv_b, 0), (mm_branch_a, 0),
             (mm_branch_b, 1), (ew_branch_a, 0), (mm_gates, 0), (ew_conv_b, 1), (ew_gates, 0),
             (mm_branch_a, 1), (mm_merge_gates, 0), (ew_scan, 0), (ew_branch_a, 1), (mm_gates, 1),
             (mm_ya_out, 0), (ew_yb, 0), (ew_gates, 1), (mm_merge_gates, 1), (ew_p_cast, 0),
             (mm_embed, 0), (mm_yb_out, 0), (ew_scan, 1), (mm_ya_out, 1), (ew_merge, 0), (ew_yb, 1),
             (mm_out, 0), (ew_p_cast, 1), (mm_embed, 1), (mm_yb_out, 1), (ew_resid, 0), (ew_merge, 1),
             (mm_embed_gate, 0), (mm_out, 1), (ew_embed, 0), (ew_resid, 1), (mm_embed_gate, 1),
             (ew_embed, 1))
    for phase, part in order:
        phase(part)

    newa_ref[...] = exta_ref[rows:rows + hist_a, :]
    newb_ref[...] = extb_ref[rows:rows + hist_b, :]
    hlast_ref[...] = h_ref[...]
    exta_ref[0:hist_a, :] = exta_ref[rows:rows + hist_a, :]
    extb_ref[0:hist_b, :] = extb_ref[rows:rows + hist_b, :]

    if batch_major_io:
        for c in out_copies(step, slot):
            c.start()

        @pl.when(step == n_steps - 1)
        def _():
            for s_back in range(min(N_SLOTS, x_in.shape[1] // tt) - 1, -1, -1):
                for c in out_copies(step - s_back, (step - s_back) % N_SLOTS):
                    c.wait()


def _run_layer(x, p, bufa0, bufb0, h0, vecs, weights, *, nb, layer, final_norm, batch_major_io):
    assert TILE_ROWS % nb == 0 and nb % SUBLANES == 0 and TILE_ROWS >= (CONV_B_WIDTH - 1) * nb
    tt = TILE_ROWS // nb
    if batch_major_io:
        assert nb == SUBLANES and x.shape[0] == nb and x.shape[1] % tt == 0
        n_steps = x.shape[1] // tt
    else:
        assert x.shape[0] % TILE_ROWS == 0
        n_steps = x.shape[0] // TILE_ROWS
    hist_a = (CONV_A_WIDTH - 1) * nb
    hist_b = (CONV_B_WIDTH - 1) * nb
    f32, bf16 = jnp.float32, jnp.bfloat16

    row_spec = lambda cols: pl.BlockSpec((TILE_ROWS, cols), lambda i: (i, 0))
    const_spec = lambda shape: pl.BlockSpec(shape, lambda i: (0,) * len(shape))
    layer_spec = lambda a: pl.BlockSpec((None,) + a.shape[1:],
                                        lambda i: (layer,) + (0,) * (a.ndim - 1),
                                        pipeline_mode=pl.Buffered(1))
    hbm = pl.BlockSpec(memory_space=pl.ANY)

    scratch = [pltpu.VMEM((TILE_ROWS, D_MODEL), bf16),
               pltpu.VMEM((N_PARTS, PART_ROWS, Z_COLS), f32),
               pltpu.VMEM((TILE_ROWS + hist_a, WIDTH), f32),
               pltpu.VMEM((TILE_ROWS + hist_b, WIDTH), f32),
               pltpu.VMEM((nb, WIDTH), f32),
               pltpu.VMEM((TILE_ROWS, WIDTH), bf16),
               pltpu.VMEM((TILE_ROWS, WIDTH), bf16),
               pltpu.VMEM((TILE_ROWS, PLE_DIM), bf16)]
    if batch_major_io:
        scratch += [pltpu.VMEM((N_SLOTS * tt, nb, D_MODEL), f32),
                    pltpu.VMEM((N_SLOTS * tt, nb, PLE_DIM), f32),
                    pltpu.VMEM((N_SLOTS * tt, nb, D_MODEL), f32),
                    pltpu.SemaphoreType.DMA((N_SLOTS,)),
                    pltpu.SemaphoreType.DMA((N_SLOTS,)),
                    pltpu.SemaphoreType.DMA((N_SLOTS,))]
        x_spec, p_spec, y_spec = hbm, hbm, hbm
        y_shape = jax.ShapeDtypeStruct(x.shape, f32)
    else:
        x_spec, p_spec, y_spec = row_spec(D_MODEL), row_spec(PLE_DIM), row_spec(D_MODEL)
        y_shape = jax.ShapeDtypeStruct(x.shape, f32)

    return pl.pallas_call(
        functools.partial(_layer_kernel, nb=nb, layer=layer, final_norm=final_norm,
                          batch_major_io=batch_major_io),
        grid=(n_steps,),
        in_specs=[x_spec, p_spec,
                  const_spec((hist_a, WIDTH)), const_spec((hist_b, WIDTH)), const_spec((nb, WIDTH)),
                  layer_spec(vecs)] + [layer_spec(w) for w in weights],
        out_specs=[y_spec, const_spec((hist_a, WIDTH)), const_spec((hist_b, WIDTH)),
                   const_spec((nb, WIDTH))],
        out_shape=[y_shape,
                   jax.ShapeDtypeStruct((hist_a, WIDTH), f32),
                   jax.ShapeDtypeStruct((hist_b, WIDTH), f32),
                   jax.ShapeDtypeStruct((nb, WIDTH), f32)],
        scratch_shapes=scratch,
        compiler_params=pltpu.CompilerParams(dimension_semantics=("arbitrary",),
                                             vmem_limit_bytes=VMEM_LIMIT_BYTES),
        name=("prompt" if batch_major_io else "sample") + f"_layer{layer}",
    )(x, p, bufa0, bufb0, h0, vecs, *weights)


def _gate_weights(w_r, w_i):
    per = MXU_DIM // LRU_BLOCK

    def superblocks(w):
        w = w.reshape(w.shape[0], N_GROUPS, per, LRU_BLOCK, LRU_BLOCK)
        eye = jnp.eye(per, dtype=w.dtype)
        return jnp.einsum('lgaij,ab->lgaibj', w, eye).reshape(w.shape[0], N_GROUPS, MXU_DIM, MXU_DIM)
    return jnp.concatenate([superblocks(w_r), superblocks(w_i)], axis=-1)


def _time_major(a):
    b, t, c = a.shape
    return jnp.swapaxes(a, 0, 1).reshape(t * b, c)


def _batch_major(a, nb):
    return jnp.swapaxes(a.reshape(a.shape[0] // nb, nb, a.shape[1]), 0, 1)


def kernel(x_prompt, x_sample, state_conv_a, state_conv_b, state_h, p_prompt, p_sample, norm_in, w_in, conv_a_w, conv_b_w, conv_b_b, w_r, b_r, w_i, b_i, lam, w_a_out, w_b_out, w_o, norm_pe, w_pg, w_pe, norm_final):
    depth = w_in.shape[0]
    n_p, n_s = x_prompt.shape[0], x_sample.shape[0]
    f32, bf16 = jnp.float32, jnp.bfloat16

    vecs = jnp.concatenate([
        norm_in[:, None], conv_a_w, conv_b_w, conv_b_b[:, None], b_r[:, None], b_i[:, None],
        lam[:, None], norm_pe[:, None], jnp.broadcast_to(norm_final, (depth, 1, D_MODEL)),
        jnp.zeros((depth, V_ROWS - 14, WIDTH), f32)], axis=1)
    half_cols = jnp.concatenate([jnp.ones((3 * WIDTH,), f32), jnp.full((WIDTH,), 0.5, f32),
                                 jnp.ones((WIDTH,), f32), jnp.full((WIDTH,), 0.5, f32),
                                 jnp.full((2 * D_MODEL,), 0.5, f32)])
    weights = ((w_in * half_cols).astype(bf16), _gate_weights(w_r, w_i).astype(bf16),
               w_a_out.astype(bf16), w_b_out.astype(bf16), (0.5 * w_o).astype(bf16),
               (0.5 * w_pg).astype(bf16), (0.5 * w_pe).astype(bf16))

    xp, xs = x_prompt, _time_major(x_sample)
    zeros_a = jnp.zeros(((CONV_A_WIDTH - 1) * n_p, WIDTH), f32)
    zeros_b = jnp.zeros(((CONV_B_WIDTH - 1) * n_p, WIDTH), f32)
    zeros_h = jnp.zeros((n_p, WIDTH), f32)
    outs = {k: [] for k in ("ca_p", "cb_p", "h_p", "ca_s", "cb_s", "h_s")}
    for l in range(depth):
        final = l == depth - 1
        xp, na, nb_, nh = _run_layer(xp, p_prompt, zeros_a, zeros_b, zeros_h, vecs, weights,
                                     nb=n_p, layer=l, final_norm=final, batch_major_io=True)
        outs["ca_p"].append(_batch_major(na, n_p))
        outs["cb_p"].append(_batch_major(nb_, n_p))
        outs["h_p"].append(nh)
        xs, na, nb_, nh = _run_layer(xs, _time_major(p_sample[l]), _time_major(state_conv_a[l]),
                                     _time_major(state_conv_b[l]), state_h[l], vecs, weights,
                                     nb=n_s, layer=l, final_norm=final, batch_major_io=False)
        outs["ca_s"].append(_batch_major(na, n_s))
        outs["cb_s"].append(_batch_major(nb_, n_s))
        outs["h_s"].append(nh)
    return (xp, _batch_major(xs, n_s),
            jnp.stack(outs["ca_p"]), jnp.stack(outs["cb_p"]), jnp.stack(outs["h_p"]),
            jnp.stack(outs["ca_s"]), jnp.stack(outs["cb_s"]), jnp.stack(outs["h_s"]))
```

```python
import functools

import jax
import jax.numpy as jnp
from jax import lax
from jax.experimental import pallas as pl
from jax.experimental.pallas import tpu as pltpu

D_MODEL = 1024
WIDTH = 1024
PLE_DIM = 256
LRU_HEADS = 16
LRU_BLOCK = WIDTH // LRU_HEADS
LRU_C = 8.0
RMS_EPS = 1e-6
CONV_A_WIDTH = 3
CONV_B_WIDTH = 4

MXU_DIM = 256
N_GROUPS = WIDTH // MXU_DIM
SUBLANES = 8
TILE_ROWS = 512
CHUNK = 16
Z_COLS = 6 * WIDTH
N_PARTS = 2
PART_ROWS = TILE_ROWS // N_PARTS
N_SLOTS = 2

V_NORM_IN, V_CONV_A, V_CONV_B, V_CONV_B_BIAS = 0, 1, 4, 8
V_B_R, V_B_I, V_LAM, V_NORM_PE, V_NORM_FINAL = 9, 10, 11, 12, 13
V_ROWS = 16

VMEM_LIMIT_BYTES = 62 * 1024 * 1024

_ONE, _HALF = 1.0, 0.5
WEIGHT_SPECS = (
    (D_MODEL, 8 * WIDTH, (_ONE, _ONE, _ONE, _HALF, _ONE, _HALF, _HALF, _HALF)),
    (WIDTH, D_MODEL, (_ONE,)),
    (WIDTH, D_MODEL, (_ONE,)),
    (D_MODEL, D_MODEL, (_HALF,)),
    (D_MODEL, D_MODEL, (_HALF,)),
    (PLE_DIM, D_MODEL, (_HALF,)),
)


def _two_sigmoid_of_twice(h):
    return jnp.tanh(h) + 1.0


def _silu_of_twice(h):
    return h + h * jnp.tanh(h)


def _softplus(y):
    return jnp.maximum(y, 0.0) + jnp.log1p(jnp.exp(-jnp.abs(y)))


def _rms_scale(x, g):
    ms = jnp.mean(x * x, axis=-1, keepdims=True)
    return x * lax.rsqrt(ms + RMS_EPS) * g


def _chunks(n_rows):
    return range(0, n_rows, CHUNK)


def _layer_kernel(x_in, p_in, bufa0_ref, bufb0_ref, h0_ref, vec_ref, w_ri_ref, *refs,
                  nb, layer, final_norm, batch_major_io):
    n_w = len(WEIGHT_SPECS)
    w_src, refs = refs[:n_w], refs[n_w:]
    y_out, newa_ref, newb_ref, hlast_ref = refs[:4]
    refs = refs[4:]
    if batch_major_io:
        w_copy_out, refs = refs[:n_w], refs[n_w:]
    hn_ref, z_ref, exta_ref, extb_ref, h_ref, ya_ref, yb_ref, pb_ref = refs[:8]
    refs = refs[8:]
    if batch_major_io:
        xbuf, pbuf, ybuf, sem_x, sem_p, sem_y = refs[:6]
        w_vmem, (sem_w, sem_wo) = refs[6:6 + n_w], refs[6 + n_w:]
    else:
        w_vmem = w_src
    w_in_ref, w_ao_ref, w_bo_ref, w_o_ref, w_pg_ref, w_pe_ref = w_vmem
    rows = TILE_ROWS
    tt = rows // nb
    hist_a = (CONV_A_WIDTH - 1) * nb
    hist_b = (CONV_B_WIDTH - 1) * nb
    f32, bf16 = jnp.float32, jnp.bfloat16
    step = pl.program_id(0)
    n_steps = pl.num_programs(0)

    if batch_major_io:
        slot = step % N_SLOTS

        def in_copies(s, sl):
            t0 = s * tt
            dst = pl.ds(sl * tt, tt)
            cps = []
            for b in range(nb):
                cps.append(pltpu.make_async_copy(x_in.at[b, pl.ds(t0, tt), :],
                                                 xbuf.at[dst, b, :], sem_x.at[sl]))
                cps.append(pltpu.make_async_copy(p_in.at[layer, b, pl.ds(t0, tt), :],
                                                 pbuf.at[dst, b, :], sem_p.at[sl]))
            return cps

        def out_copies(s, sl):
            src = pl.ds(sl * tt, tt)
            return [pltpu.make_async_copy(ybuf.at[src, b, :], y_out.at[b, pl.ds(s * tt, tt), :],
                                          sem_y.at[sl]) for b in range(nb)]

        @pl.when(step == 0)
        def _():
            for c in in_copies(0, 0):
                c.start()

        @pl.when(step + 1 < n_steps)
        def _():
            for c in in_copies(step + 1, (step + 1) % N_SLOTS):
                c.start()

        for c in in_copies(step, slot):
            c.wait()

        @pl.when(step >= N_SLOTS)
        def _():
            for c in out_copies(step - N_SLOTS, slot):
                c.wait()

        half_t = tt * N_SLOTS // 2
        piece_rows = half_t * nb
        pieces = []
        for wi, (n_rows, n_cols, col_scale) in enumerate(WEIGHT_SPECS):
            for cb in range(n_cols // D_MODEL):
                for rb in range(pl.cdiv(n_rows, piece_rows)):
                    pieces.append((wi, rb, cb, min(piece_rows, n_rows - rb * piece_rows), col_scale[cb]))

        def piece_copy(k):
            wi, rb, cb, n_rows, _ = pieces[k]
            return pltpu.make_async_copy(
                w_src[wi].at[layer, pl.ds(rb * half_t, n_rows // nb), :, pl.ds(cb * D_MODEL, D_MODEL)],
                ybuf.at[pl.ds((k % 2) * half_t, n_rows // nb), :, :], sem_w.at[k % 2])

        def piece_convert(k):
            wi, rb, cb, n_rows, scale = pieces[k]

            def body(i, carry):
                v = ybuf[pl.ds((k % 2) * half_t + i * (CHUNK // nb), CHUNK // nb), :, :]
                v = v.reshape(CHUNK, D_MODEL) * scale
                r0 = pl.multiple_of(rb * piece_rows + i * CHUNK, CHUNK)
                w_vmem[wi][pl.ds(r0, CHUNK), cb * D_MODEL:(cb + 1) * D_MODEL] = v.astype(bf16)
                return carry
            lax.fori_loop(0, n_rows // CHUNK, body, 0)

        def weight_out_copies():
            return [pltpu.make_async_copy(w_vmem[wi], w_copy_out[wi], sem_wo.at[wi]) for wi in range(n_w)]

        @pl.when(step == 0)
        def _():
            piece_copy(0).start()
            for k in range(len(pieces)):
                if k + 1 < len(pieces):
                    piece_copy(k + 1).start()
                piece_copy(k).wait()
                piece_convert(k)
            for c in weight_out_copies():
                c.start()

        def staged(r0, n):
            return pl.ds(slot * tt + r0 // nb, n // nb)

        def load_x(r0, n):
            return xbuf[staged(r0, n), :, :].reshape(n, D_MODEL)

        def load_p(r0, n):
            return pbuf[staged(r0, n), :, :].reshape(n, PLE_DIM)

        def load_y(r0, n):
            return ybuf[staged(r0, n), :, :].reshape(n, D_MODEL)

        def store_y(r0, n, val):
            ybuf[staged(r0, n), :, :] = val.reshape(n // nb, nb, D_MODEL)
    else:
        def load_x(r0, n):
            return x_in[pl.ds(r0, n), :]

        def load_p(r0, n):
            return p_in[pl.ds(r0, n), :]

        def load_y(r0, n):
            return y_out[pl.ds(r0, n), :]

        def store_y(r0, n, val):
            y_out[pl.ds(r0, n), :] = val

    @pl.when(step == 0)
    def _():
        exta_ref[0:hist_a, :] = bufa0_ref[...]
        extb_ref[0:hist_b, :] = bufb0_ref[...]
        h_ref[...] = h0_ref[...]

    def vec(row, n=1):
        return vec_ref[row:row + n, :]

    g_in, g_pe, g_fin = vec(V_NORM_IN), vec(V_NORM_PE), vec(V_NORM_FINAL)
    wa = [vec(V_CONV_A + k) for k in range(CONV_A_WIDTH)]
    wb = [0.5 * vec(V_CONV_B + k) for k in range(CONV_B_WIDTH)]
    bias_b = 0.5 * vec(V_CONV_B_BIAS)
    half_neg_c_sp = (-0.5 * LRU_C) * _softplus(-vec(V_LAM))
    half_b_r, half_b_i = 0.5 * vec(V_B_R), 0.5 * vec(V_B_I)
    xcb_ref = yb_ref

    COL_B = 0 * WIDTH
    COL_A = 2 * WIDTH
    COL_MERGE = COL_A
    COL_GATE = COL_A + 2 * WIDTH

    def gate_cols(j):
        c0 = COL_GATE + 2 * MXU_DIM * j
        return slice(c0, c0 + MXU_DIM), slice(c0 + MXU_DIM, c0 + 2 * MXU_DIM)

    def rows_of(part):
        return pl.ds(part * PART_ROWS, PART_ROWS)

    def dot(a, b):
        return jnp.dot(a, b, preferred_element_type=f32)

    def ew_norm_in(part):
        g0 = part * PART_ROWS
        for r0 in _chunks(PART_ROWS):
            hn_ref[pl.ds(g0 + r0, CHUNK), :] = _rms_scale(load_x(g0 + r0, CHUNK), g_in).astype(bf16)

    def mm_branch_b(part):
        z_ref[part, :, COL_B:COL_B + 2 * WIDTH] = dot(hn_ref[rows_of(part), :],
                                                      w_in_ref[:, 4 * WIDTH:6 * WIDTH])

    def mm_branch_a(part):
        z_ref[part, :, COL_A:COL_A + 4 * WIDTH] = dot(hn_ref[rows_of(part), :],
                                                      w_in_ref[:, 0:4 * WIDTH])

    def ew_conv_b(part):
        g0, z = part * PART_ROWS, z_ref.at[part]
        for r0 in _chunks(PART_ROWS):
            sl = pl.ds(r0, CHUNK)
            xb = z[sl, COL_B:COL_B + WIDTH]
            extb_ref[pl.ds(g0 + r0 + hist_b, CHUNK), :] = xb
            xh = xb * wb[CONV_B_WIDTH - 1] + bias_b
            for k in range(CONV_B_WIDTH - 1):
                xh = xh + extb_ref[pl.ds(g0 + r0 + k * nb, CHUNK), :] * wb[k]
            z[sl, COL_B:COL_B + WIDTH] = xh
            xcb_ref[pl.ds(g0 + r0, CHUNK), :] = xh.astype(bf16)

    def ew_branch_a(part):
        g0, z = part * PART_ROWS, z_ref.at[part]
        for r0 in _chunks(PART_ROWS):
            sl = pl.ds(r0, CHUNK)
            v = z[sl, COL_A + 2 * WIDTH:COL_A + 3 * WIDTH] * z[sl, COL_A:COL_A + WIDTH]
            exta_ref[pl.ds(g0 + r0 + hist_a, CHUNK), :] = v
            conv = v * wa[CONV_A_WIDTH - 1]
            for k in range(CONV_A_WIDTH - 1):
                conv = conv + exta_ref[pl.ds(g0 + r0 + k * nb, CHUNK), :] * wa[k]
            ya = (z[sl, COL_A + WIDTH:COL_A + 2 * WIDTH] * conv
                  * _silu_of_twice(z[sl, COL_A + 3 * WIDTH:COL_A + 4 * WIDTH]))
            ya_ref[pl.ds(g0 + r0, CHUNK), :] = ya.astype(bf16)

    def mm_gates(part):
        for j in range(N_GROUPS):
            z_ref[part, :, COL_GATE + 2 * MXU_DIM * j:COL_GATE + 2 * MXU_DIM * (j + 1)] = dot(
                xcb_ref[rows_of(part), MXU_DIM * j:MXU_DIM * (j + 1)], w_ri_ref[j])

    def mm_merge_gates(part):
        z_ref[part, :, COL_MERGE:COL_MERGE + 2 * WIDTH] = dot(hn_ref[rows_of(part), :],
                                                              w_in_ref[:, 6 * WIDTH:8 * WIDTH])

    def ew_gates(part):
        z = z_ref.at[part]
        for r0 in _chunks(PART_ROWS):
            sl = pl.ds(r0, CHUNK)
            for j in range(N_GROUPS):
                cols = slice(MXU_DIM * j, MXU_DIM * (j + 1))
                rcols, icols = gate_cols(j)
                t_r = jnp.tanh(z[sl, rcols] + half_b_r[:, cols])
                gi2 = _two_sigmoid_of_twice(z[sl, icols] + half_b_i[:, cols])
                log_a = t_r * half_neg_c_sp[:, cols] + half_neg_c_sp[:, cols]
                a = jnp.exp(log_a)
                s = jnp.tanh(log_a) * (-1.0 - a * a)
                beta = jnp.where(s > 0.0, s * lax.rsqrt(s), 0.0)
                z[sl, rcols] = a
                z[sl, icols] = beta * gi2 * z[sl, COL_B + MXU_DIM * j:COL_B + MXU_DIM * (j + 1)]

    def ew_scan(part):
        z = z_ref.at[part]
        for b0 in range(0, nb, SUBLANES):
            for j in range(N_GROUPS):
                acols, ucols = gate_cols(j)
                hcols = slice(MXU_DIM * j, MXU_DIM * (j + 1))
                h = h_ref[b0:b0 + SUBLANES, hcols]
                for t in range(PART_ROWS // nb):
                    rsl = pl.ds(t * nb + b0, SUBLANES)
                    h = z[rsl, acols] * h + z[rsl, ucols]
                    z[rsl, ucols] = h
                h_ref[b0:b0 + SUBLANES, hcols] = h

    def mm_ya_out(part):
        z_ref[part, :, COL_B:COL_B + WIDTH] = dot(ya_ref[rows_of(part), :], w_ao_ref[...])

    def ew_yb(part):
        g0, z = part * PART_ROWS, z_ref.at[part]
        for r0 in _chunks(PART_ROWS):
            sl = pl.ds(r0, CHUNK)
            sg = _silu_of_twice(z[sl, COL_B + WIDTH:COL_B + 2 * WIDTH])
            for j in range(N_GROUPS):
                cols = slice(MXU_DIM * j, MXU_DIM * (j + 1))
                yb_ref[pl.ds(g0 + r0, CHUNK), cols] = (z[sl, gate_cols(j)[1]] * sg[:, cols]).astype(bf16)

    def ew_p_cast(part):
        g0 = part * PART_ROWS
        for r0 in _chunks(PART_ROWS):
            pb_ref[pl.ds(g0 + r0, CHUNK), :] = load_p(g0 + r0, CHUNK).astype(bf16)

    def mm_embed(part):
        z_ref[part, :, COL_GATE:COL_GATE + WIDTH] = dot(pb_ref[rows_of(part), :], w_pe_ref[...])

    def mm_yb_out(part):
        z_ref[part, :, COL_B + WIDTH:COL_B + 2 * WIDTH] = dot(yb_ref[rows_of(part), :], w_bo_ref[...])

    def ew_merge(part):
        g0, z = part * PART_ROWS, z_ref.at[part]
        for r0 in _chunks(PART_ROWS):
            sl = pl.ds(r0, CHUNK)
            m = (_two_sigmoid_of_twice(z[sl, COL_MERGE:COL_MERGE + WIDTH]) * z[sl, COL_B:COL_B + WIDTH]
                 + _two_sigmoid_of_twice(z[sl, COL_MERGE + WIDTH:COL_MERGE + 2 * WIDTH])
                 * z[sl, COL_B + WIDTH:COL_B + 2 * WIDTH])
            ya_ref[pl.ds(g0 + r0, CHUNK), :] = m.astype(bf16)

    def mm_out(part):
        z_ref[part, :, COL_MERGE:COL_MERGE + WIDTH] = dot(ya_ref[rows_of(part), :], w_o_ref[...])

    def ew_resid(part):
        g0, z = part * PART_ROWS, z_ref.at[part]
        for r0 in _chunks(PART_ROWS):
            x1 = load_x(g0 + r0, CHUNK) + z[pl.ds(r0, CHUNK), COL_MERGE:COL_MERGE + WIDTH]
            store_y(g0 + r0, CHUNK, x1)
            hn_ref[pl.ds(g0 + r0, CHUNK), :] = _rms_scale(x1, g_pe).astype(bf16)

    def mm_embed_gate(part):
        z_ref[part, :, COL_MERGE + WIDTH:COL_MERGE + 2 * WIDTH] = dot(hn_ref[rows_of(part), :],
                                                                      w_pg_ref[...])

    def ew_embed(part):
        g0, z = part * PART_ROWS, z_ref.at[part]
        for r0 in _chunks(PART_ROWS):
            sl = pl.ds(r0, CHUNK)
            x2 = load_y(g0 + r0, CHUNK) + (
                z[sl, COL_GATE:COL_GATE + WIDTH]
                * _two_sigmoid_of_twice(z[sl, COL_MERGE + WIDTH:COL_MERGE + 2 * WIDTH]))
            store_y(g0 + r0, CHUNK, _rms_scale(x2, g_fin) if final_norm else x2)

    assert N_PARTS == 2
    order = ((ew_norm_in, 0), (mm_branch_b, 0), (ew_norm_in, 1), (ew_conv_b, 0), (mm_branch_a, 0),
             (mm_branch_b, 1), (ew_branch_a, 0), (mm_gates, 0), (ew_conv_b, 1), (ew_gates, 0),
             (mm_branch_a, 1), (mm_merge_gates, 0), (ew_scan, 0), (ew_branch_a, 1), (mm_gates, 1),
             (mm_ya_out, 0), (ew_yb, 0), (ew_gates, 1), (mm_merge_gates, 1), (ew_p_cast, 0),
             (mm_embed, 0), (mm_yb_out, 0), (ew_scan, 1), (mm_ya_out, 1), (ew_merge, 0), (ew_yb, 1),
             (mm_out, 0), (ew_p_cast, 1), (mm_embed, 1), (mm_yb_out, 1), (ew_resid, 0), (ew_merge, 1),
             (mm_embed_gate, 0), (mm_out, 1), (ew_embed, 0), (ew_resid, 1), (mm_embed_gate, 1),
             (ew_embed, 1))
    for phase, part in order:
        phase(part)

    newa_ref[...] = exta_ref[rows:rows + hist_a, :]
    newb_ref[...] = extb_ref[rows:rows + hist_b, :]
    hlast_ref[...] = h_ref[...]
    exta_ref[0:hist_a, :] = exta_ref[rows:rows + hist_a, :]
    extb_ref[0:hist_b, :] = extb_ref[rows:rows + hist_b, :]

    if batch_major_io:
        for c in out_copies(step, slot):
            c.start()

        @pl.when(step == n_steps - 1)
        def _():
            for c in weight_out_copies():
                c.wait()
            for s_back in range(min(N_SLOTS, x_in.shape[1] // tt) - 1, -1, -1):
                for c in out_copies(step - s_back, (step - s_back) % N_SLOTS):
                    c.wait()


def _run_layer(x, p, bufa0, bufb0, h0, vecs, w_ri, big_weights, *, nb, layer, final_norm,
               batch_major_io):
    assert TILE_ROWS % nb == 0 and nb % SUBLANES == 0 and TILE_ROWS >= (CONV_B_WIDTH - 1) * nb
    tt = TILE_ROWS // nb
    if batch_major_io:
        assert nb == SUBLANES and x.shape[0] == nb and x.shape[1] % tt == 0
        n_steps = x.shape[1] // tt
    else:
        assert x.shape[0] % TILE_ROWS == 0
        n_steps = x.shape[0] // TILE_ROWS
    hist_a = (CONV_A_WIDTH - 1) * nb
    hist_b = (CONV_B_WIDTH - 1) * nb
    f32, bf16 = jnp.float32, jnp.bfloat16

    row_spec = lambda cols: pl.BlockSpec((TILE_ROWS, cols), lambda i: (i, 0))
    const_spec = lambda shape: pl.BlockSpec(shape, lambda i: (0,) * len(shape))
    layer_spec = lambda a: pl.BlockSpec((None,) + a.shape[1:],
                                        lambda i: (layer,) + (0,) * (a.ndim - 1),
                                        pipeline_mode=pl.Buffered(1))
    hbm = pl.BlockSpec(memory_space=pl.ANY)
    vmem = pl.BlockSpec(memory_space=pltpu.VMEM)
    w_shapes = [(r, c) for r, c, _ in WEIGHT_SPECS]

    scratch = [pltpu.VMEM((TILE_ROWS, D_MODEL), bf16),
               pltpu.VMEM((N_PARTS, PART_ROWS, Z_COLS), f32),
               pltpu.VMEM((TILE_ROWS + hist_a, WIDTH), f32),
               pltpu.VMEM((TILE_ROWS + hist_b, WIDTH), f32),
               pltpu.VMEM((nb, WIDTH), f32),
               pltpu.VMEM((TILE_ROWS, WIDTH), bf16),
               pltpu.VMEM((TILE_ROWS, WIDTH), bf16),
               pltpu.VMEM((TILE_ROWS, PLE_DIM), bf16)]
    out_specs = [None, const_spec((hist_a, WIDTH)), const_spec((hist_b, WIDTH)), const_spec((nb, WIDTH))]
    out_shape = [jax.ShapeDtypeStruct(x.shape, f32),
                 jax.ShapeDtypeStruct((hist_a, WIDTH), f32),
                 jax.ShapeDtypeStruct((hist_b, WIDTH), f32),
                 jax.ShapeDtypeStruct((nb, WIDTH), f32)]
    if batch_major_io:
        scratch += [pltpu.VMEM((N_SLOTS * tt, nb, D_MODEL), f32),
                    pltpu.VMEM((N_SLOTS * tt, nb, PLE_DIM), f32),
                    pltpu.VMEM((N_SLOTS * tt, nb, D_MODEL), f32),
                    pltpu.SemaphoreType.DMA((N_SLOTS,)),
                    pltpu.SemaphoreType.DMA((N_SLOTS,)),
                    pltpu.SemaphoreType.DMA((N_SLOTS,))]
        scratch += [pltpu.VMEM(shape, bf16) for shape in w_shapes]
        scratch += [pltpu.SemaphoreType.DMA((2,)), pltpu.SemaphoreType.DMA((len(w_shapes),))]
        x_spec, p_spec, out_specs[0], w_specs = hbm, hbm, hbm, [hbm] * len(w_shapes)
        out_specs += [hbm] * len(w_shapes)
        out_shape += [jax.ShapeDtypeStruct(shape, bf16) for shape in w_shapes]
    else:
        x_spec, p_spec, out_specs[0] = row_spec(D_MODEL), row_spec(PLE_DIM), row_spec(D_MODEL)
        w_specs = [vmem] * len(w_shapes)

    return pl.pallas_call(
        functools.partial(_layer_kernel, nb=nb, layer=layer, final_norm=final_norm,
                          batch_major_io=batch_major_io),
        grid=(n_steps,),
        in_specs=[x_spec, p_spec,
                  const_spec((hist_a, WIDTH)), const_spec((hist_b, WIDTH)), const_spec((nb, WIDTH)),
                  layer_spec(vecs), layer_spec(w_ri)] + w_specs,
        out_specs=out_specs,
        out_shape=out_shape,
        scratch_shapes=scratch,
        compiler_params=pltpu.CompilerParams(dimension_semantics=("arbitrary",),
                                             vmem_limit_bytes=VMEM_LIMIT_BYTES),
        name=("prompt" if batch_major_io else "sample") + f"_layer{layer}",
    )(x, p, bufa0, bufb0, h0, vecs, w_ri, *big_weights)


def _gate_weights(w_r, w_i):
    per = MXU_DIM // LRU_BLOCK

    def superblocks(w):
        w = w.reshape(w.shape[0], N_GROUPS, per, LRU_BLOCK, LRU_BLOCK)
        eye = jnp.eye(per, dtype=w.dtype)
        return jnp.einsum('lgaij,ab->lgaibj', w, eye).reshape(w.shape[0], N_GROUPS, MXU_DIM, MXU_DIM)
    return jnp.concatenate([superblocks(w_r), superblocks(w_i)], axis=-1)


def _time_major(a):
    b, t, c = a.shape
    return jnp.swapaxes(a, 0, 1).reshape(t * b, c)


def _batch_major(a, nb):
    return jnp.swapaxes(a.reshape(a.shape[0] // nb, nb, a.shape[1]), 0, 1)


def kernel(x_prompt, x_sample, state_conv_a, state_conv_b, state_h, p_prompt, p_sample, norm_in, w_in, conv_a_w, conv_b_w, conv_b_b, w_r, b_r, w_i, b_i, lam, w_a_out, w_b_out, w_o, norm_pe, w_pg, w_pe, norm_final):
    depth = w_in.shape[0]
    n_p, n_s = x_prompt.shape[0], x_sample.shape[0]
    f32, bf16 = jnp.float32, jnp.bfloat16

    vecs = jnp.concatenate([
        norm_in[:, None], conv_a_w, conv_b_w, conv_b_b[:, None], b_r[:, None], b_i[:, None],
        lam[:, None], norm_pe[:, None], jnp.broadcast_to(norm_final, (depth, 1, D_MODEL)),
        jnp.zeros((depth, V_ROWS - 14, WIDTH), f32)], axis=1)
    w_ri = _gate_weights(w_r, w_i).astype(bf16)
    big_f32 = tuple(w.reshape(depth, w.shape[1] // SUBLANES, SUBLANES, w.shape[2])
                    for w in (w_in, w_a_out, w_b_out, w_o, w_pg, w_pe))

    xp, xs = x_prompt, _time_major(x_sample)
    zeros_a = jnp.zeros(((CONV_A_WIDTH - 1) * n_p, WIDTH), f32)
    zeros_b = jnp.zeros(((CONV_B_WIDTH - 1) * n_p, WIDTH), f32)
    zeros_h = jnp.zeros((n_p, WIDTH), f32)
    outs = {k: [] for k in ("ca_p", "cb_p", "h_p", "ca_s", "cb_s", "h_s")}
    for l in range(depth):
        final = l == depth - 1
        xp, na, nb_, nh, *big_bf16 = _run_layer(xp, p_prompt, zeros_a, zeros_b, zeros_h, vecs, w_ri, big_f32,
                                                nb=n_p, layer=l, final_norm=final, batch_major_io=True)
        outs["ca_p"].append(_batch_major(na, n_p))
        outs["cb_p"].append(_batch_major(nb_, n_p))
        outs["h_p"].append(nh)
        xs, na, nb_, nh = _run_layer(xs, _time_major(p_sample[l]), _time_major(state_conv_a[l]),
                                     _time_major(state_conv_b[l]), state_h[l], vecs, w_ri, big_bf16,
                                     nb=n_s, layer=l, final_norm=final, batch_major_io=False)
        outs["ca_s"].append(_batch_major(na, n_s))
        outs["cb_s"].append(_batch_major(nb_, n_s))
        outs["h_s"].append(nh)
    return (xp, _batch_major(xs, n_s),
            jnp.stack(outs["ca_p"]), jnp.stack(outs["cb_p"]), jnp.stack(outs["h_p"]),
            jnp.stack(outs["ca_s"]), jnp.stack(outs["cb_s"]), jnp.stack(outs["h_s"]))
```

```python
import functools

import jax
import jax.numpy as jnp
from jax import lax
from jax.experimental import pallas as pl
from jax.experimental.pallas import tpu as pltpu

D_MODEL = 1024
WIDTH = 1024
PLE_DIM = 256
LRU_HEADS = 16
LRU_BLOCK = WIDTH // LRU_HEADS
LRU_C = 8.0
RMS_EPS = 1e-6
CONV_A_WIDTH = 3
CONV_B_WIDTH = 4

MXU_DIM = 256
N_GROUPS = WIDTH // MXU_DIM
SUBLANES = 8
TILE_ROWS = 512
CHUNK = 16
Z_COLS = 6 * WIDTH
N_PARTS = 2
PART_ROWS = TILE_ROWS // N_PARTS
N_SLOTS = 2

CHANNEL_PARAMS = ("norm_in", "conv_a_w", "conv_b_w", "conv_b_b", "b_r", "b_i", "lam", "norm_pe",
                  "norm_final")

VMEM_LIMIT_BYTES = 62 * 1024 * 1024

_ONE, _HALF = 1.0, 0.5
WEIGHT_SPECS = (
    (D_MODEL, 8 * WIDTH, (_ONE, _ONE, _ONE, _HALF, _ONE, _HALF, _HALF, _HALF)),
    (WIDTH, D_MODEL, (_ONE,)),
    (WIDTH, D_MODEL, (_ONE,)),
    (D_MODEL, D_MODEL, (_HALF,)),
    (D_MODEL, D_MODEL, (_HALF,)),
    (PLE_DIM, D_MODEL, (_HALF,)),
)


def _two_sigmoid_of_twice(h):
    return jnp.tanh(h) + 1.0


def _silu_of_twice(h):
    return h + h * jnp.tanh(h)


def _softplus(y):
    return jnp.maximum(y, 0.0) + jnp.log1p(jnp.exp(-jnp.abs(y)))


def _rms_scale(x, g):
    ms = jnp.mean(x * x, axis=-1, keepdims=True)
    return x * lax.rsqrt(ms + RMS_EPS) * g


def _chunks(n_rows):
    return range(0, n_rows, CHUNK)


def _layer_kernel(x_in, p_in, bufa0_ref, bufb0_ref, h0_ref, *refs,
                  nb, layer, final_norm, batch_major_io):
    chan = dict(zip(CHANNEL_PARAMS, refs))
    w_ri_ref, refs = refs[len(CHANNEL_PARAMS)], refs[len(CHANNEL_PARAMS) + 1:]
    n_w = len(WEIGHT_SPECS)
    w_src, refs = refs[:n_w], refs[n_w:]
    y_out, newa_ref, newb_ref, hlast_ref = refs[:4]
    refs = refs[4:]
    if batch_major_io:
        w_copy_out, refs = refs[:n_w], refs[n_w:]
    hn_ref, z_ref, exta_ref, extb_ref, h_ref, ya_ref, yb_ref, pb_ref = refs[:8]
    refs = refs[8:]
    if batch_major_io:
        xbuf, pbuf, ybuf, sem_x, sem_p, sem_y = refs[:6]
        w_vmem, (sem_w, sem_wo) = refs[6:6 + n_w], refs[6 + n_w:]
    else:
        w_vmem = w_src
    w_in_ref, w_ao_ref, w_bo_ref, w_o_ref, w_pg_ref, w_pe_ref = w_vmem
    rows = TILE_ROWS
    tt = rows // nb
    hist_a = (CONV_A_WIDTH - 1) * nb
    hist_b = (CONV_B_WIDTH - 1) * nb
    f32, bf16 = jnp.float32, jnp.bfloat16
    step = pl.program_id(0)
    n_steps = pl.num_programs(0)

    if batch_major_io:
        slot = step % N_SLOTS

        def in_copies(s, sl):
            t0 = s * tt
            dst = pl.ds(sl * tt, tt)
            cps = []
            for b in range(nb):
                cps.append(pltpu.make_async_copy(x_in.at[b, pl.ds(t0, tt), :],
                                                 xbuf.at[dst, b, :], sem_x.at[sl]))
                cps.append(pltpu.make_async_copy(p_in.at[layer, b, pl.ds(t0, tt), :],
                                                 pbuf.at[dst, b, :], sem_p.at[sl]))
            return cps

        def out_copies(s, sl):
            src = pl.ds(sl * tt, tt)
            return [pltpu.make_async_copy(ybuf.at[src, b, :], y_out.at[b, pl.ds(s * tt, tt), :],
                                          sem_y.at[sl]) for b in range(nb)]

        @pl.when(step == 0)
        def _():
            for c in in_copies(0, 0):
                c.start()

        @pl.when(step + 1 < n_steps)
        def _():
            for c in in_copies(step + 1, (step + 1) % N_SLOTS):
                c.start()

        for c in in_copies(step, slot):
            c.wait()

        @pl.when(step >= N_SLOTS)
        def _():
            for c in out_copies(step - N_SLOTS, slot):
                c.wait()

        half_t = tt * N_SLOTS // 2
        piece_rows = half_t * nb
        pieces = []
        for wi, (n_rows, n_cols, col_scale) in enumerate(WEIGHT_SPECS):
            for cb in range(n_cols // D_MODEL):
                for rb in range(pl.cdiv(n_rows, piece_rows)):
                    pieces.append((wi, rb, cb, min(piece_rows, n_rows - rb * piece_rows), col_scale[cb]))

        def piece_copy(k):
            wi, rb, cb, n_rows, _ = pieces[k]
            return pltpu.make_async_copy(
                w_src[wi].at[layer, pl.ds(rb * half_t, n_rows // nb), :, pl.ds(cb * D_MODEL, D_MODEL)],
                ybuf.at[pl.ds((k % 2) * half_t, n_rows // nb), :, :], sem_w.at[k % 2])

        def piece_convert(k):
            wi, rb, cb, n_rows, scale = pieces[k]

            def body(i, carry):
                v = ybuf[pl.ds((k % 2) * half_t + i * (CHUNK // nb), CHUNK // nb), :, :]
                v = v.reshape(CHUNK, D_MODEL) * scale
                r0 = pl.multiple_of(rb * piece_rows + i * CHUNK, CHUNK)
                w_vmem[wi][pl.ds(r0, CHUNK), cb * D_MODEL:(cb + 1) * D_MODEL] = v.astype(bf16)
                return carry
            lax.fori_loop(0, n_rows // CHUNK, body, 0, unroll=8)

        def weight_out_copies():
            return [pltpu.make_async_copy(w_vmem[wi], w_copy_out[wi], sem_wo.at[wi]) for wi in range(n_w)]

        @pl.when(step == 0)
        def _():
            piece_copy(0).start()
            for k in range(len(pieces)):
                if k + 1 < len(pieces):
                    piece_copy(k + 1).start()
                piece_copy(k).wait()
                piece_convert(k)
            for c in weight_out_copies():
                c.start()

        def staged(r0, n):
            return pl.ds(slot * tt + r0 // nb, n // nb)

        def load_x(r0, n):
            return xbuf[staged(r0, n), :, :].reshape(n, D_MODEL)

        def load_p(r0, n):
            return pbuf[staged(r0, n), :, :].reshape(n, PLE_DIM)

        def load_y(r0, n):
            return ybuf[staged(r0, n), :, :].reshape(n, D_MODEL)

        def store_y(r0, n, val):
            ybuf[staged(r0, n), :, :] = val.reshape(n // nb, nb, D_MODEL)
    else:
        def load_x(r0, n):
            return x_in[pl.ds(r0, n), :]

        def load_p(r0, n):
            return p_in[pl.ds(r0, n), :]

        def load_y(r0, n):
            return y_out[pl.ds(r0, n), :]

        def store_y(r0, n, val):
            y_out[pl.ds(r0, n), :] = val

    @pl.when(step == 0)
    def _():
        exta_ref[0:hist_a, :] = bufa0_ref[...]
        extb_ref[0:hist_b, :] = bufb0_ref[...]
        h_ref[...] = h0_ref[...]

    def vec(name, row=0):
        return chan[name][row:row + 1, :]

    g_in, g_pe, g_fin = vec("norm_in"), vec("norm_pe"), vec("norm_final")
    wa = [vec("conv_a_w", k) for k in range(CONV_A_WIDTH)]
    wb = [0.5 * vec("conv_b_w", k) for k in range(CONV_B_WIDTH)]
    bias_b = 0.5 * vec("conv_b_b")
    half_neg_c_sp = (-0.5 * LRU_C) * _softplus(-vec("lam"))
    half_b_r, half_b_i = 0.5 * vec("b_r"), 0.5 * vec("b_i")
    xcb_ref = yb_ref

    COL_B = 0 * WIDTH
    COL_A = 2 * WIDTH
    COL_MERGE = COL_A
    COL_GATE = COL_A + 2 * WIDTH

    def gate_cols(j):
        c0 = COL_GATE + 2 * MXU_DIM * j
        return slice(c0, c0 + MXU_DIM), slice(c0 + MXU_DIM, c0 + 2 * MXU_DIM)

    def rows_of(part):
        return pl.ds(part * PART_ROWS, PART_ROWS)

    def dot(a, b):
        return jnp.dot(a, b, preferred_element_type=f32)

    def ew_norm_in(part):
        g0 = part * PART_ROWS
        for r0 in _chunks(PART_ROWS):
            hn_ref[pl.ds(g0 + r0, CHUNK), :] = _rms_scale(load_x(g0 + r0, CHUNK), g_in).astype(bf16)

    def mm_branch_b(part):
        z_ref[part, :, COL_B:COL_B + 2 * WIDTH] = dot(hn_ref[rows_of(part), :],
                                                      w_in_ref[:, 4 * WIDTH:6 * WIDTH])

    def mm_branch_a(part):
        z_ref[part, :, COL_A:COL_A + 4 * WIDTH] = dot(hn_ref[rows_of(part), :],
                                                      w_in_ref[:, 0:4 * WIDTH])

    def ew_conv_b(part):
        g0, z = part * PART_ROWS, z_ref.at[part]
        for r0 in _chunks(PART_ROWS):
            sl = pl.ds(r0, CHUNK)
            xb = z[sl, COL_B:COL_B + WIDTH]
            extb_ref[pl.ds(g0 + r0 + hist_b, CHUNK), :] = xb
            xh = xb * wb[CONV_B_WIDTH - 1] + bias_b
            for k in range(CONV_B_WIDTH - 1):
                xh = xh + extb_ref[pl.ds(g0 + r0 + k * nb, CHUNK), :] * wb[k]
            z[sl, COL_B:COL_B + WIDTH] = xh
            xcb_ref[pl.ds(g0 + r0, CHUNK), :] = xh.astype(bf16)

    def ew_branch_a(part):
        g0, z = part * PART_ROWS, z_ref.at[part]
        for r0 in _chunks(PART_ROWS):
            sl = pl.ds(r0, CHUNK)
            v = z[sl, COL_A + 2 * WIDTH:COL_A + 3 * WIDTH] * z[sl, COL_A:COL_A + WIDTH]
            exta_ref[pl.ds(g0 + r0 + hist_a, CHUNK), :] = v
            conv = v * wa[CONV_A_WIDTH - 1]
            for k in range(CONV_A_WIDTH - 1):
                conv = conv + exta_ref[pl.ds(g0 + r0 + k * nb, CHUNK), :] * wa[k]
            ya = (z[sl, COL_A + WIDTH:COL_A + 2 * WIDTH] * conv
                  * _silu_of_twice(z[sl, COL_A + 3 * WIDTH:COL_A + 4 * WIDTH]))
            ya_ref[pl.ds(g0 + r0, CHUNK), :] = ya.astype(bf16)

    def mm_gates(part):
        for j in range(N_GROUPS):
            z_ref[part, :, COL_GATE + 2 * MXU_DIM * j:COL_GATE + 2 * MXU_DIM * (j + 1)] = dot(
                xcb_ref[rows_of(part), MXU_DIM * j:MXU_DIM * (j + 1)], w_ri_ref[j])

    def mm_merge_gates(part):
        z_ref[part, :, COL_MERGE:COL_MERGE + 2 * WIDTH] = dot(hn_ref[rows_of(part), :],
                                                              w_in_ref[:, 6 * WIDTH:8 * WIDTH])

    def ew_gates(part):
        z = z_ref.at[part]
        for r0 in _chunks(PART_ROWS):
            sl = pl.ds(r0, CHUNK)
            for j in range(N_GROUPS):
                cols = slice(MXU_DIM * j, MXU_DIM * (j + 1))
                rcols, icols = gate_cols(j)
                t_r = jnp.tanh(z[sl, rcols] + half_b_r[:, cols])
                gi2 = _two_sigmoid_of_twice(z[sl, icols] + half_b_i[:, cols])
                log_a = t_r * half_neg_c_sp[:, cols] + half_neg_c_sp[:, cols]
                a = jnp.exp(log_a)
                s = jnp.tanh(log_a) * (-1.0 - a * a)
                beta = jnp.where(s > 0.0, s * lax.rsqrt(s), 0.0)
                z[sl, rcols] = a
                z[sl, icols] = beta * gi2 * z[sl, COL_B + MXU_DIM * j:COL_B + MXU_DIM * (j + 1)]

    def ew_scan(part):
        z = z_ref.at[part]
        for b0 in range(0, nb, SUBLANES):
            for j in range(N_GROUPS):
                acols, ucols = gate_cols(j)
                hcols = slice(MXU_DIM * j, MXU_DIM * (j + 1))
                h = h_ref[b0:b0 + SUBLANES, hcols]
                for t in range(PART_ROWS // nb):
                    rsl = pl.ds(t * nb + b0, SUBLANES)
                    h = z[rsl, acols] * h + z[rsl, ucols]
                    z[rsl, ucols] = h
                h_ref[b0:b0 + SUBLANES, hcols] = h

    def mm_ya_out(part):
        z_ref[part, :, COL_B:COL_B + WIDTH] = dot(ya_ref[rows_of(part), :], w_ao_ref[...])

    def ew_yb(part):
        g0, z = part * PART_ROWS, z_ref.at[part]
        for r0 in _chunks(PART_ROWS):
            sl = pl.ds(r0, CHUNK)
            sg = _silu_of_twice(z[sl, COL_B + WIDTH:COL_B + 2 * WIDTH])
            for j in range(N_GROUPS):
                cols = slice(MXU_DIM * j, MXU_DIM * (j + 1))
                yb_ref[pl.ds(g0 + r0, CHUNK), cols] = (z[sl, gate_cols(j)[1]] * sg[:, cols]).astype(bf16)

    def ew_p_cast(part):
        g0 = part * PART_ROWS
        for r0 in _chunks(PART_ROWS):
            pb_ref[pl.ds(g0 + r0, CHUNK), :] = load_p(g0 + r0, CHUNK).astype(bf16)

    def mm_embed(part):
        z_ref[part, :, COL_GATE:COL_GATE + WIDTH] = dot(pb_ref[rows_of(part), :], w_pe_ref[...])

    def mm_yb_out(part):
        z_ref[part, :, COL_B + WIDTH:COL_B + 2 * WIDTH] = dot(yb_ref[rows_of(part), :], w_bo_ref[...])

    def ew_merge(part):
        g0, z = part * PART_ROWS, z_ref.at[part]
        for r0 in _chunks(PART_ROWS):
            sl = pl.ds(r0, CHUNK)
            m = (_two_sigmoid_of_twice(z[sl, COL_MERGE:COL_MERGE + WIDTH]) * z[sl, COL_B:COL_B + WIDTH]
                 + _two_sigmoid_of_twice(z[sl, COL_MERGE + WIDTH:COL_MERGE + 2 * WIDTH])
                 * z[sl, COL_B + WIDTH:COL_B + 2 * WIDTH])
            ya_ref[pl.ds(g0 + r0, CHUNK), :] = m.astype(bf16)

    def mm_out(part):
        z_ref[part, :, COL_MERGE:COL_MERGE + WIDTH] = dot(ya_ref[rows_of(part), :], w_o_ref[...])

    def ew_resid(part):
        g0, z = part * PART_ROWS, z_ref.at[part]
        for r0 in _chunks(PART_ROWS):
            x1 = load_x(g0 + r0, CHUNK) + z[pl.ds(r0, CHUNK), COL_MERGE:COL_MERGE + WIDTH]
            store_y(g0 + r0, CHUNK, x1)
            hn_ref[pl.ds(g0 + r0, CHUNK), :] = _rms_scale(x1, g_pe).astype(bf16)

    def mm_embed_gate(part):
        z_ref[part, :, COL_MERGE + WIDTH:COL_MERGE + 2 * WIDTH] = dot(hn_ref[rows_of(part), :],
                                                                      w_pg_ref[...])

    def ew_embed(part):
        g0, z = part * PART_ROWS, z_ref.at[part]
        for r0 in _chunks(PART_ROWS):
            sl = pl.ds(r0, CHUNK)
            x2 = load_y(g0 + r0, CHUNK) + (
                z[sl, COL_GATE:COL_GATE + WIDTH]
                * _two_sigmoid_of_twice(z[sl, COL_MERGE + WIDTH:COL_MERGE + 2 * WIDTH]))
            store_y(g0 + r0, CHUNK, _rms_scale(x2, g_fin) if final_norm else x2)

    assert N_PARTS == 2
    order = ((ew_norm_in, 0), (mm_branch_b, 0), (ew_norm_in, 1), (ew_conv_b, 0), (mm_branch_a, 0),
             (mm_branch_b, 1), (ew_branch_a, 0), (mm_gates, 0), (ew_conv_b, 1), (ew_gates, 0),
             (mm_branch_a, 1), (mm_merge_gates, 0), (ew_scan, 0), (ew_branch_a, 1), (mm_gates, 1),
             (mm_ya_out, 0), (ew_yb, 0), (ew_gates, 1), (mm_merge_gates, 1), (ew_p_cast, 0),
             (mm_embed, 0), (mm_yb_out, 0), (ew_scan, 1), (mm_ya_out, 1), (ew_merge, 0), (ew_yb, 1),
             (mm_out, 0), (ew_p_cast, 1), (mm_embed, 1), (mm_yb_out, 1), (ew_resid, 0), (ew_merge, 1),
             (mm_embed_gate, 0), (mm_out, 1), (ew_embed, 0), (ew_resid, 1), (mm_embed_gate, 1),
             (ew_embed, 1))
    for phase, part in order:
        phase(part)

    newa_ref[...] = exta_ref[rows:rows + hist_a, :]
    newb_ref[...] = extb_ref[rows:rows + hist_b, :]
    hlast_ref[...] = h_ref[...]
    exta_ref[0:hist_a, :] = exta_ref[rows:rows + hist_a, :]
    extb_ref[0:hist_b, :] = extb_ref[rows:rows + hist_b, :]

    if batch_major_io:
        for c in out_copies(step, slot):
            c.start()

        @pl.when(step == n_steps - 1)
        def _():
            for c in weight_out_copies():
                c.wait()
            for s_back in range(min(N_SLOTS, x_in.shape[1] // tt) - 1, -1, -1):
                for c in out_copies(step - s_back, (step - s_back) % N_SLOTS):
                    c.wait()


def _run_layer(x, p, bufa0, bufb0, h0, chans, w_ri, big_weights, *, nb, layer, final_norm,
               batch_major_io):
    assert TILE_ROWS % nb == 0 and nb % SUBLANES == 0 and TILE_ROWS >= (CONV_B_WIDTH - 1) * nb
    tt = TILE_ROWS // nb
    if batch_major_io:
        assert nb == SUBLANES and x.shape[0] == nb and x.shape[1] % tt == 0
        n_steps = x.shape[1] // tt
    else:
        assert x.shape[0] % TILE_ROWS == 0
        n_steps = x.shape[0] // TILE_ROWS
    hist_a = (CONV_A_WIDTH - 1) * nb
    hist_b = (CONV_B_WIDTH - 1) * nb
    f32, bf16 = jnp.float32, jnp.bfloat16

    row_spec = lambda cols: pl.BlockSpec((TILE_ROWS, cols), lambda i: (i, 0))
    const_spec = lambda shape: pl.BlockSpec(shape, lambda i: (0,) * len(shape))
    layer_spec = lambda a: pl.BlockSpec((None,) + a.shape[1:],
                                        lambda i: (min(layer, a.shape[0] - 1),) + (0,) * (a.ndim - 1),
                                        pipeline_mode=pl.Buffered(1))
    hbm = pl.BlockSpec(memory_space=pl.ANY)
    vmem = pl.BlockSpec(memory_space=pltpu.VMEM)
    w_shapes = [(r, c) for r, c, _ in WEIGHT_SPECS]

    scratch = [pltpu.VMEM((TILE_ROWS, D_MODEL), bf16),
               pltpu.VMEM((N_PARTS, PART_ROWS, Z_COLS), f32),
               pltpu.VMEM((TILE_ROWS + hist_a, WIDTH), f32),
               pltpu.VMEM((TILE_ROWS + hist_b, WIDTH), f32),
               pltpu.VMEM((nb, WIDTH), f32),
               pltpu.VMEM((TILE_ROWS, WIDTH), bf16),
               pltpu.VMEM((TILE_ROWS, WIDTH), bf16),
               pltpu.VMEM((TILE_ROWS, PLE_DIM), bf16)]
    out_specs = [None, const_spec((hist_a, WIDTH)), const_spec((hist_b, WIDTH)), const_spec((nb, WIDTH))]
    out_shape = [jax.ShapeDtypeStruct(x.shape, f32),
                 jax.ShapeDtypeStruct((hist_a, WIDTH), f32),
                 jax.ShapeDtypeStruct((hist_b, WIDTH), f32),
                 jax.ShapeDtypeStruct((nb, WIDTH), f32)]
    if batch_major_io:
        scratch += [pltpu.VMEM((N_SLOTS * tt, nb, D_MODEL), f32),
                    pltpu.VMEM((N_SLOTS * tt, nb, PLE_DIM), f32),
                    pltpu.VMEM((N_SLOTS * tt, nb, D_MODEL), f32),
                    pltpu.SemaphoreType.DMA((N_SLOTS,)),
                    pltpu.SemaphoreType.DMA((N_SLOTS,)),
                    pltpu.SemaphoreType.DMA((N_SLOTS,))]
        scratch += [pltpu.VMEM(shape, bf16) for shape in w_shapes]
        scratch += [pltpu.SemaphoreType.DMA((2,)), pltpu.SemaphoreType.DMA((len(w_shapes),))]
        x_spec, p_spec, out_specs[0], w_specs = hbm, hbm, hbm, [hbm] * len(w_shapes)
        out_specs += [hbm] * len(w_shapes)
        out_shape += [jax.ShapeDtypeStruct(shape, bf16) for shape in w_shapes]
    else:
        x_spec, p_spec, out_specs[0] = row_spec(D_MODEL), row_spec(PLE_DIM), row_spec(D_MODEL)
        w_specs = [vmem] * len(w_shapes)

    return pl.pallas_call(
        functools.partial(_layer_kernel, nb=nb, layer=layer, final_norm=final_norm,
                          batch_major_io=batch_major_io),
        grid=(n_steps,),
        in_specs=[x_spec, p_spec,
                  const_spec((hist_a, WIDTH)), const_spec((hist_b, WIDTH)), const_spec((nb, WIDTH)),
                  *[layer_spec(c) for c in chans], layer_spec(w_ri)] + w_specs,
        out_specs=out_specs,
        out_shape=out_shape,
        scratch_shapes=scratch,
        compiler_params=pltpu.CompilerParams(dimension_semantics=("arbitrary",),
                                             vmem_limit_bytes=VMEM_LIMIT_BYTES),
        name=("prompt" if batch_major_io else "sample") + f"_layer{layer}",
    )(x, p, bufa0, bufb0, h0, *chans, w_ri, *big_weights)


def _gate_weights(w_r, w_i):
    per = MXU_DIM // LRU_BLOCK

    def superblocks(w):
        w = w.reshape(w.shape[0], N_GROUPS, per, LRU_BLOCK, LRU_BLOCK)
        eye = jnp.eye(per, dtype=w.dtype)
        return jnp.einsum('lgaij,ab->lgaibj', w, eye).reshape(w.shape[0], N_GROUPS, MXU_DIM, MXU_DIM)
    return jnp.concatenate([superblocks(w_r), superblocks(w_i)], axis=-1)


def _time_major(a):
    b, t, c = a.shape
    return jnp.swapaxes(a, 0, 1).reshape(t * b, c)


def _batch_major(a, nb):
    return jnp.swapaxes(a.reshape(a.shape[0] // nb, nb, a.shape[1]), 0, 1)


def kernel(x_prompt, x_sample, state_conv_a, state_conv_b, state_h, p_prompt, p_sample, norm_in, w_in, conv_a_w, conv_b_w, conv_b_b, w_r, b_r, w_i, b_i, lam, w_a_out, w_b_out, w_o, norm_pe, w_pg, w_pe, norm_final):
    depth = w_in.shape[0]
    n_p, n_s = x_prompt.shape[0], x_sample.shape[0]
    f32, bf16 = jnp.float32, jnp.bfloat16

    by_name = dict(norm_in=norm_in, conv_a_w=conv_a_w, conv_b_w=conv_b_w, conv_b_b=conv_b_b, b_r=b_r,
                   b_i=b_i, lam=lam, norm_pe=norm_pe, norm_final=norm_final[None])
    chans = tuple(by_name[k].reshape(by_name[k].shape[0], -1, WIDTH) for k in CHANNEL_PARAMS)
    w_ri = _gate_weights(w_r, w_i).astype(bf16)
    big_f32 = tuple(w.reshape(depth, w.shape[1] // SUBLANES, SUBLANES, w.shape[2])
                    for w in (w_in, w_a_out, w_b_out, w_o, w_pg, w_pe))

    xp, xs = x_prompt, _time_major(x_sample)
    zeros_a = jnp.zeros(((CONV_A_WIDTH - 1) * n_p, WIDTH), f32)
    zeros_b = jnp.zeros(((CONV_B_WIDTH - 1) * n_p, WIDTH), f32)
    zeros_h = jnp.zeros((n_p, WIDTH), f32)
    outs = {k: [] for k in ("ca_p", "cb_p", "h_p", "ca_s", "cb_s", "h_s")}
    for l in range(depth):
        final = l == depth - 1
        xp, na, nb_, nh, *big_bf16 = _run_layer(xp, p_prompt, zeros_a, zeros_b, zeros_h, chans, w_ri, big_f32,
                                                nb=n_p, layer=l, final_norm=final, batch_major_io=True)
        outs["ca_p"].append(na)
        outs["cb_p"].append(nb_)
        outs["h_p"].append(nh)
        xs, na, nb_, nh = _run_layer(xs, _time_major(p_sample[l]), _time_major(state_conv_a[l]),
                                     _time_major(state_conv_b[l]), state_h[l], chans, w_ri, big_bf16,
                                     nb=n_s, layer=l, final_norm=final, batch_major_io=False)
        outs["ca_s"].append(na)
        outs["cb_s"].append(nb_)
        outs["h_s"].append(nh)

    def stacked_batch_major(per_layer, nb):
        a = jnp.stack(per_layer)
        return jnp.swapaxes(a.reshape(depth, -1, nb, a.shape[-1]), 1, 2)

    return (xp, _batch_major(xs, n_s),
            stacked_batch_major(outs["ca_p"], n_p), stacked_batch_major(outs["cb_p"], n_p),
            jnp.stack(outs["h_p"]),
            stacked_batch_major(outs["ca_s"], n_s), stacked_batch_major(outs["cb_s"], n_s),
            jnp.stack(outs["h_s"]))
```

```python
import functools

import jax
import jax.numpy as jnp
from jax import lax
from jax.experimental import pallas as pl
from jax.experimental.pallas import tpu as pltpu

D_MODEL = 1024
WIDTH = 1024
PLE_DIM = 256
LRU_HEADS = 16
LRU_BLOCK = WIDTH // LRU_HEADS
LRU_C = 8.0
RMS_EPS = 1e-6
CONV_A_WIDTH = 3
CONV_B_WIDTH = 4

MXU_DIM = 256
N_GROUPS = WIDTH // MXU_DIM
SUBLANES = 8
TILE_ROWS = 512
CHUNK = 16
Z_COLS = 6 * WIDTH
N_PARTS = 2
PART_ROWS = TILE_ROWS // N_PARTS
N_SLOTS = 2

V_NORM_IN, V_CONV_A, V_CONV_B, V_CONV_B_BIAS = 0, 1, 4, 8
V_B_R, V_B_I, V_LAM, V_NORM_PE, V_NORM_FINAL = 9, 10, 11, 12, 13
V_ROWS = 16

VMEM_LIMIT_BYTES = 62 * 1024 * 1024

_ONE, _HALF = 1.0, 0.5
WEIGHT_SPECS = (
    (D_MODEL, 8 * WIDTH, (_ONE, _ONE, _ONE, _HALF, _ONE, _HALF, _HALF, _HALF)),
    (WIDTH, D_MODEL, (_ONE,)),
    (WIDTH, D_MODEL, (_ONE,)),
    (D_MODEL, D_MODEL, (_HALF,)),
    (D_MODEL, D_MODEL, (_HALF,)),
    (PLE_DIM, D_MODEL, (_HALF,)),
)


def _two_sigmoid_of_twice(h):
    return jnp.tanh(h) + 1.0


def _silu_of_twice(h):
    return h + h * jnp.tanh(h)


def _softplus(y):
    return jnp.maximum(y, 0.0) + jnp.log1p(jnp.exp(-jnp.abs(y)))


def _rms_scale(x, g):
    ms = jnp.mean(x * x, axis=-1, keepdims=True)
    return x * lax.rsqrt(ms + RMS_EPS) * g


def _chunks(n_rows):
    return range(0, n_rows, CHUNK)


def _layer_kernel(x_in, p_in, bufa0_ref, bufb0_ref, h0_ref, vec_ref, w_ri_ref, *refs,
                  nb, layer, final_norm, batch_major_io):
    n_w = len(WEIGHT_SPECS)
    w_src, refs = refs[:n_w], refs[n_w:]
    y_out, newa_ref, newb_ref, hlast_ref = refs[:4]
    refs = refs[4:]
    if batch_major_io:
        w_copy_out, refs = refs[:n_w], refs[n_w:]
    hn_ref, z_ref, exta_ref, extb_ref, h_ref, ya_ref, yb_ref, pb_ref = refs[:8]
    refs = refs[8:]
    if batch_major_io:
        xbuf, pbuf, ybuf, sem_x, sem_p, sem_y = refs[:6]
        w_vmem, (sem_w, sem_wo) = refs[6:6 + n_w], refs[6 + n_w:]
    else:
        w_vmem = w_src
    w_in_ref, w_ao_ref, w_bo_ref, w_o_ref, w_pg_ref, w_pe_ref = w_vmem
    rows = TILE_ROWS
    tt = rows // nb
    hist_a = (CONV_A_WIDTH - 1) * nb
    hist_b = (CONV_B_WIDTH - 1) * nb
    f32, bf16 = jnp.float32, jnp.bfloat16
    step = pl.program_id(0)
    n_steps = pl.num_programs(0)

    if batch_major_io:
        slot = step % N_SLOTS

        def in_copies(s, sl):
            t0 = s * tt
            dst = pl.ds(sl * tt, tt)
            cps = []
            for b in range(nb):
                cps.append(pltpu.make_async_copy(x_in.at[b, pl.ds(t0, tt), :],
                                                 xbuf.at[dst, b, :], sem_x.at[sl]))
                cps.append(pltpu.make_async_copy(p_in.at[layer, b, pl.ds(t0, tt), :],
                                                 pbuf.at[dst, b, :], sem_p.at[sl]))
            return cps

        def out_copies(s, sl):
            src = pl.ds(sl * tt, tt)
            return [pltpu.make_async_copy(ybuf.at[src, b, :], y_out.at[b, pl.ds(s * tt, tt), :],
                                          sem_y.at[sl]) for b in range(nb)]

        @pl.when(step == 0)
        def _():
            for c in in_copies(0, 0):
                c.start()

        @pl.when(step + 1 < n_steps)
        def _():
            for c in in_copies(step + 1, (step + 1) % N_SLOTS):
                c.start()

        for c in in_copies(step, slot):
            c.wait()

        @pl.when(step >= N_SLOTS)
        def _():
            for c in out_copies(step - N_SLOTS, slot):
                c.wait()

        pieces = []
        for wi, (n_rows, n_cols, _) in enumerate(WEIGHT_SPECS):
            for r0 in range(0, n_rows, PART_ROWS):
                for c0 in range(0, n_cols, Z_COLS):
                    pieces.append((wi, r0, min(PART_ROWS, n_rows - r0), c0, min(Z_COLS, n_cols - c0)))

        def piece_copy(k):
            wi, r0, n_rows, c0, n_cols = pieces[k]
            return pltpu.make_async_copy(
                w_src[wi].at[layer, pl.ds(r0, n_rows), pl.ds(c0, n_cols)],
                z_ref.at[k % N_PARTS, pl.ds(0, n_rows), pl.ds(0, n_cols)], sem_w.at[k % N_PARTS])

        def piece_convert(k):
            wi, r0, n_rows, c0, n_cols = pieces[k]
            col_scale = WEIGHT_SPECS[wi][2]

            def body(i, carry):
                rsl = pl.ds(pl.multiple_of(i * CHUNK, CHUNK), CHUNK)
                for c in range(0, n_cols, D_MODEL):
                    v = z_ref[k % N_PARTS, rsl, c:c + D_MODEL] * col_scale[(c0 + c) // D_MODEL]
                    w_vmem[wi][pl.ds(pl.multiple_of(r0 + i * CHUNK, CHUNK), CHUNK),
                               c0 + c:c0 + c + D_MODEL] = v.astype(bf16)
                return carry
            lax.fori_loop(0, n_rows // CHUNK, body, 0)

        def weight_out_copies():
            return [pltpu.make_async_copy(w_vmem[wi], w_copy_out[wi], sem_wo.at[wi]) for wi in range(n_w)]

        @pl.when(step == 0)
        def _():
            piece_copy(0).start()
            for k in range(len(pieces)):
                if k + 1 < len(pieces):
                    piece_copy(k + 1).start()
                piece_copy(k).wait()
                piece_convert(k)
            for c in weight_out_copies():
                c.start()

        def staged(r0, n):
            return pl.ds(slot * tt + r0 // nb, n // nb)

        def load_x(r0, n):
            return xbuf[staged(r0, n), :, :].reshape(n, D_MODEL)

        def load_p(r0, n):
            return pbuf[staged(r0, n), :, :].reshape(n, PLE_DIM)

        def load_y(r0, n):
            return ybuf[staged(r0, n), :, :].reshape(n, D_MODEL)

        def store_y(r0, n, val):
            ybuf[staged(r0, n), :, :] = val.reshape(n // nb, nb, D_MODEL)
    else:
        def load_x(r0, n):
            return x_in[pl.ds(r0, n), :]

        def load_p(r0, n):
            return p_in[pl.ds(r0, n), :]

        def load_y(r0, n):
            return y_out[pl.ds(r0, n), :]

        def store_y(r0, n, val):
            y_out[pl.ds(r0, n), :] = val

    @pl.when(step == 0)
    def _():
        exta_ref[0:hist_a, :] = bufa0_ref[...]
        extb_ref[0:hist_b, :] = bufb0_ref[...]
        h_ref[...] = h0_ref[...]

    def vec(row, n=1):
        return vec_ref[row:row + n, :]

    g_in, g_pe, g_fin = vec(V_NORM_IN), vec(V_NORM_PE), vec(V_NORM_FINAL)
    wa = [vec(V_CONV_A + k) for k in range(CONV_A_WIDTH)]
    wb = [0.5 * vec(V_CONV_B + k) for k in range(CONV_B_WIDTH)]
    bias_b = 0.5 * vec(V_CONV_B_BIAS)
    half_neg_c_sp = (-0.5 * LRU_C) * _softplus(-vec(V_LAM))
    half_b_r, half_b_i = 0.5 * vec(V_B_R), 0.5 * vec(V_B_I)
    xcb_ref = yb_ref

    COL_B = 0 * WIDTH
    COL_A = 2 * WIDTH
    COL_MERGE = COL_A
    COL_GATE = COL_A + 2 * WIDTH

    def gate_cols(j):
        c0 = COL_GATE + 2 * MXU_DIM * j
        return slice(c0, c0 + MXU_DIM), slice(c0 + MXU_DIM, c0 + 2 * MXU_DIM)

    def rows_of(part):
        return pl.ds(part * PART_ROWS, PART_ROWS)

    def dot(a, b):
        return jnp.dot(a, b, preferred_element_type=f32)

    def ew_norm_in(part):
        g0 = part * PART_ROWS
        for r0 in _chunks(PART_ROWS):
            hn_ref[pl.ds(g0 + r0, CHUNK), :] = _rms_scale(load_x(g0 + r0, CHUNK), g_in).astype(bf16)

    def mm_branch_b(part):
        z_ref[part, :, COL_B:COL_B + 2 * WIDTH] = dot(hn_ref[rows_of(part), :],
                                                      w_in_ref[:, 4 * WIDTH:6 * WIDTH])

    def mm_branch_a(part):
        z_ref[part, :, COL_A:COL_A + 4 * WIDTH] = dot(hn_ref[rows_of(part), :],
                                                      w_in_ref[:, 0:4 * WIDTH])

    def ew_conv_b(part):
        g0, z = part * PART_ROWS, z_ref.at[part]
        for r0 in _chunks(PART_ROWS):
            sl = pl.ds(r0, CHUNK)
            xb = z[sl, COL_B:COL_B + WIDTH]
            extb_ref[pl.ds(g0 + r0 + hist_b, CHUNK), :] = xb
            xh = xb * wb[CONV_B_WIDTH - 1] + bias_b
            for k in range(CONV_B_WIDTH - 1):
                xh = xh + extb_ref[pl.ds(g0 + r0 + k * nb, CHUNK), :] * wb[k]
            z[sl, COL_B:COL_B + WIDTH] = xh
            xcb_ref[pl.ds(g0 + r0, CHUNK), :] = xh.astype(bf16)

    def ew_branch_a(part):
        g0, z = part * PART_ROWS, z_ref.at[part]
        for r0 in _chunks(PART_ROWS):
            sl = pl.ds(r0, CHUNK)
            v = z[sl, COL_A + 2 * WIDTH:COL_A + 3 * WIDTH] * z[sl, COL_A:COL_A + WIDTH]
            exta_ref[pl.ds(g0 + r0 + hist_a, CHUNK), :] = v
            conv = v * wa[CONV_A_WIDTH - 1]
            for k in range(CONV_A_WIDTH - 1):
                conv = conv + exta_ref[pl.ds(g0 + r0 + k * nb, CHUNK), :] * wa[k]
            ya = (z[sl, COL_A + WIDTH:COL_A + 2 * WIDTH] * conv
                  * _silu_of_twice(z[sl, COL_A + 3 * WIDTH:COL_A + 4 * WIDTH]))
            ya_ref[pl.ds(g0 + r0, CHUNK), :] = ya.astype(bf16)

    def mm_gates(part):
        for j in range(N_GROUPS):
            z_ref[part, :, COL_GATE + 2 * MXU_DIM * j:COL_GATE + 2 * MXU_DIM * (j + 1)] = dot(
                xcb_ref[rows_of(part), MXU_DIM * j:MXU_DIM * (j + 1)], w_ri_ref[j])

    def mm_merge_gates(part):
        z_ref[part, :, COL_MERGE:COL_MERGE + 2 * WIDTH] = dot(hn_ref[rows_of(part), :],
                                                              w_in_ref[:, 6 * WIDTH:8 * WIDTH])

    def ew_gates(part):
        z = z_ref.at[part]
        for r0 in _chunks(PART_ROWS):
            sl = pl.ds(r0, CHUNK)
            for j in range(N_GROUPS):
                cols = slice(MXU_DIM * j, MXU_DIM * (j + 1))
                rcols, icols = gate_cols(j)
                t_r = jnp.tanh(z[sl, rcols] + half_b_r[:, cols])
                gi2 = _two_sigmoid_of_twice(z[sl, icols] + half_b_i[:, cols])
                log_a = t_r * half_neg_c_sp[:, cols] + half_neg_c_sp[:, cols]
                a = jnp.exp(log_a)
                s = jnp.tanh(log_a) * (-1.0 - a * a)
                beta = jnp.where(s > 0.0, s * lax.rsqrt(s), 0.0)
                z[sl, rcols] = a
                z[sl, icols] = beta * gi2 * z[sl, COL_B + MXU_DIM * j:COL_B + MXU_DIM * (j + 1)]

    def ew_scan(part):
        z = z_ref.at[part]
        for b0 in range(0, nb, SUBLANES):
            for j in range(N_GROUPS):
                acols, ucols = gate_cols(j)
                hcols = slice(MXU_DIM * j, MXU_DIM * (j + 1))
                h = h_ref[b0:b0 + SUBLANES, hcols]
                for t in range(PART_ROWS // nb):
                    rsl = pl.ds(t * nb + b0, SUBLANES)
                    h = z[rsl, acols] * h + z[rsl, ucols]
                    z[rsl, ucols] = h
                h_ref[b0:b0 + SUBLANES, hcols] = h

    def mm_ya_out(part):
        z_ref[part, :, COL_B:COL_B + WIDTH] = dot(ya_ref[rows_of(part), :], w_ao_ref[...])

    def ew_yb(part):
        g0, z = part * PART_ROWS, z_ref.at[part]
        for r0 in _chunks(PART_ROWS):
            sl = pl.ds(r0, CHUNK)
            sg = _silu_of_twice(z[sl, COL_B + WIDTH:COL_B + 2 * WIDTH])
            for j in range(N_GROUPS):
                cols = slice(MXU_DIM * j, MXU_DIM * (j + 1))
                yb_ref[pl.ds(g0 + r0, CHUNK), cols] = (z[sl, gate_cols(j)[1]] * sg[:, cols]).astype(bf16)

    def ew_p_cast(part):
        g0 = part * PART_ROWS
        for r0 in _chunks(PART_ROWS):
            pb_ref[pl.ds(g0 + r0, CHUNK), :] = load_p(g0 + r0, CHUNK).astype(bf16)

    def mm_embed(part):
        z_ref[part, :, COL_GATE:COL_GATE + WIDTH] = dot(pb_ref[rows_of(part), :], w_pe_ref[...])

    def mm_yb_out(part):
        z_ref[part, :, COL_B + WIDTH:COL_B + 2 * WIDTH] = dot(yb_ref[rows_of(part), :], w_bo_ref[...])

    def ew_merge(part):
        g0, z = part * PART_ROWS, z_ref.at[part]
        for r0 in _chunks(PART_ROWS):
            sl = pl.ds(r0, CHUNK)
            m = (_two_sigmoid_of_twice(z[sl, COL_MERGE:COL_MERGE + WIDTH]) * z[sl, COL_B:COL_B + WIDTH]
                 + _two_sigmoid_of_twice(z[sl, COL_MERGE + WIDTH:COL_MERGE + 2 * WIDTH])
                 * z[sl, COL_B + WIDTH:COL_B + 2 * WIDTH])
            ya_ref[pl.ds(g0 + r0, CHUNK), :] = m.astype(bf16)

    def mm_out(part):
        z_ref[part, :, COL_MERGE:COL_MERGE + WIDTH] = dot(ya_ref[rows_of(part), :], w_o_ref[...])

    def ew_resid(part):
        g0, z = part * PART_ROWS, z_ref.at[part]
        for r0 in _chunks(PART_ROWS):
            x1 = load_x(g0 + r0, CHUNK) + z[pl.ds(r0, CHUNK), COL_MERGE:COL_MERGE + WIDTH]
            store_y(g0 + r0, CHUNK, x1)
            hn_ref[pl.ds(g0 + r0, CHUNK), :] = _rms_scale(x1, g_pe).astype(bf16)

    def mm_embed_gate(part):
        z_ref[part, :, COL_MERGE + WIDTH:COL_MERGE + 2 * WIDTH] = dot(hn_ref[rows_of(part), :],
                                                                      w_pg_ref[...])

    def ew_embed(part):
        g0, z = part * PART_ROWS, z_ref.at[part]
        for r0 in _chunks(PART_ROWS):
            sl = pl.ds(r0, CHUNK)
            x2 = load_y(g0 + r0, CHUNK) + (
                z[sl, COL_GATE:COL_GATE + WIDTH]
                * _two_sigmoid_of_twice(z[sl, COL_MERGE + WIDTH:COL_MERGE + 2 * WIDTH]))
            store_y(g0 + r0, CHUNK, _rms_scale(x2, g_fin) if final_norm else x2)

    assert N_PARTS == 2
    order = ((ew_norm_in, 0), (mm_branch_b, 0), (ew_norm_in, 1), (ew_conv_b, 0), (mm_branch_a, 0),
             (mm_branch_b, 1), (ew_branch_a, 0), (mm_gates, 0), (ew_conv_b, 1), (ew_gates, 0),
             (mm_branch_a, 1), (mm_merge_gates, 0), (ew_scan, 0), (ew_branch_a, 1), (mm_gates, 1),
             (mm_ya_out, 0), (ew_yb, 0), (ew_gates, 1), (mm_merge_gates, 1), (ew_p_cast, 0),
             (mm_embed, 0), (mm_yb_out, 0), (ew_scan, 1), (mm_ya_out, 1), (ew_merge, 0), (ew_yb, 1),
             (mm_out, 0), (ew_p_cast, 1), (mm_embed, 1), (mm_yb_out, 1), (ew_resid, 0), (ew_merge, 1),
             (mm_embed_gate, 0), (mm_out, 1), (ew_embed, 0), (ew_resid, 1), (mm_embed_gate, 1),
             (ew_embed, 1))
    for phase, part in order:
        phase(part)

    newa_ref[...] = exta_ref[rows:rows + hist_a, :]
    newb_ref[...] = extb_ref[rows:rows + hist_b, :]
    hlast_ref[...] = h_ref[...]
    exta_ref[0:hist_a, :] = exta_ref[rows:rows + hist_a, :]
    extb_ref[0:hist_b, :] = extb_ref[rows:rows + hist_b, :]

    if batch_major_io:
        for c in out_copies(step, slot):
            c.start()

        @pl.when(step == n_steps - 1)
        def _():
            for c in weight_out_copies():
                c.wait()
            for s_back in range(min(N_SLOTS, x_in.shape[1] // tt) - 1, -1, -1):
                for c in out_copies(step - s_back, (step - s_back) % N_SLOTS):
                    c.wait()


def _run_layer(x, p, bufa0, bufb0, h0, vecs, w_ri, big_weights, *, nb, layer, final_norm,
               batch_major_io):
    assert TILE_ROWS % nb == 0 and nb % SUBLANES == 0 and TILE_ROWS >= (CONV_B_WIDTH - 1) * nb
    tt = TILE_ROWS // nb
    if batch_major_io:
        assert nb == SUBLANES and x.shape[0] == nb and x.shape[1] % tt == 0
        n_steps = x.shape[1] // tt
    else:
        assert x.shape[0] % TILE_ROWS == 0
        n_steps = x.shape[0] // TILE_ROWS
    hist_a = (CONV_A_WIDTH - 1) * nb
    hist_b = (CONV_B_WIDTH - 1) * nb
    f32, bf16 = jnp.float32, jnp.bfloat16

    row_spec = lambda cols: pl.BlockSpec((TILE_ROWS, cols), lambda i: (i, 0))
    const_spec = lambda shape: pl.BlockSpec(shape, lambda i: (0,) * len(shape))
    layer_spec = lambda a: pl.BlockSpec((None,) + a.shape[1:],
                                        lambda i: (layer,) + (0,) * (a.ndim - 1),
                                        pipeline_mode=pl.Buffered(1))
    hbm = pl.BlockSpec(memory_space=pl.ANY)
    vmem = pl.BlockSpec(memory_space=pltpu.VMEM)
    w_shapes = [(r, c) for r, c, _ in WEIGHT_SPECS]

    scratch = [pltpu.VMEM((TILE_ROWS, D_MODEL), bf16),
               pltpu.VMEM((N_PARTS, PART_ROWS, Z_COLS), f32),
               pltpu.VMEM((TILE_ROWS + hist_a, WIDTH), f32),
               pltpu.VMEM((TILE_ROWS + hist_b, WIDTH), f32),
               pltpu.VMEM((nb, WIDTH), f32),
               pltpu.VMEM((TILE_ROWS, WIDTH), bf16),
               pltpu.VMEM((TILE_ROWS, WIDTH), bf16),
               pltpu.VMEM((TILE_ROWS, PLE_DIM), bf16)]
    out_specs = [None, const_spec((hist_a, WIDTH)), const_spec((hist_b, WIDTH)), const_spec((nb, WIDTH))]
    out_shape = [jax.ShapeDtypeStruct(x.shape, f32),
                 jax.ShapeDtypeStruct((hist_a, WIDTH), f32),
                 jax.ShapeDtypeStruct((hist_b, WIDTH), f32),
                 jax.ShapeDtypeStruct((nb, WIDTH), f32)]
    if batch_major_io:
        scratch += [pltpu.VMEM((N_SLOTS * tt, nb, D_MODEL), f32),
                    pltpu.VMEM((N_SLOTS * tt, nb, PLE_DIM), f32),
                    pltpu.VMEM((N_SLOTS * tt, nb, D_MODEL), f32),
                    pltpu.SemaphoreType.DMA((N_SLOTS,)),
                    pltpu.SemaphoreType.DMA((N_SLOTS,)),
                    pltpu.SemaphoreType.DMA((N_SLOTS,))]
        scratch += [pltpu.VMEM(shape, bf16) for shape in w_shapes]
        scratch += [pltpu.SemaphoreType.DMA((N_PARTS,)), pltpu.SemaphoreType.DMA((len(w_shapes),))]
        x_spec, p_spec, out_specs[0], w_specs = hbm, hbm, hbm, [hbm] * len(w_shapes)
        out_specs += [hbm] * len(w_shapes)
        out_shape += [jax.ShapeDtypeStruct(shape, bf16) for shape in w_shapes]
    else:
        x_spec, p_spec, out_specs[0] = row_spec(D_MODEL), row_spec(PLE_DIM), row_spec(D_MODEL)
        w_specs = [vmem] * len(w_shapes)

    return pl.pallas_call(
        functools.partial(_layer_kernel, nb=nb, layer=layer, final_norm=final_norm,
                          batch_major_io=batch_major_io),
        grid=(n_steps,),
        in_specs=[x_spec, p_spec,
                  const_spec((hist_a, WIDTH)), const_spec((hist_b, WIDTH)), const_spec((nb, WIDTH)),
                  layer_spec(vecs), layer_spec(w_ri)] + w_specs,
        out_specs=out_specs,
        out_shape=out_shape,
        scratch_shapes=scratch,
        compiler_params=pltpu.CompilerParams(dimension_semantics=("arbitrary",),
                                             vmem_limit_bytes=VMEM_LIMIT_BYTES),
        name=("prompt" if batch_major_io else "sample") + f"_layer{layer}",
    )(x, p, bufa0, bufb0, h0, vecs, w_ri, *big_weights)


def _gate_weights(w_r, w_i):
    per = MXU_DIM // LRU_BLOCK

    def superblocks(w):
        w = w.reshape(w.shape[0], N_GROUPS, per, LRU_BLOCK, LRU_BLOCK)
        eye = jnp.eye(per, dtype=w.dtype)
        return jnp.einsum('lgaij,ab->lgaibj', w, eye).reshape(w.shape[0], N_GROUPS, MXU_DIM, MXU_DIM)
    return jnp.concatenate([superblocks(w_r), superblocks(w_i)], axis=-1)


def _time_major(a):
    b, t, c = a.shape
    return jnp.swapaxes(a, 0, 1).reshape(t * b, c)


def _batch_major(a, nb):
    return jnp.swapaxes(a.reshape(a.shape[0] // nb, nb, a.shape[1]), 0, 1)


def kernel(x_prompt, x_sample, state_conv_a, state_conv_b, state_h, p_prompt, p_sample, norm_in, w_in, conv_a_w, conv_b_w, conv_b_b, w_r, b_r, w_i, b_i, lam, w_a_out, w_b_out, w_o, norm_pe, w_pg, w_pe, norm_final):
    depth = w_in.shape[0]
    n_p, n_s = x_prompt.shape[0], x_sample.shape[0]
    f32, bf16 = jnp.float32, jnp.bfloat16

    vecs = jnp.concatenate([
        norm_in[:, None], conv_a_w, conv_b_w, conv_b_b[:, None], b_r[:, None], b_i[:, None],
        lam[:, None], norm_pe[:, None], jnp.broadcast_to(norm_final, (depth, 1, D_MODEL)),
        jnp.zeros((depth, V_ROWS - 14, WIDTH), f32)], axis=1)
    w_ri = _gate_weights(w_r, w_i).astype(bf16)
    big_f32 = (w_in, w_a_out, w_b_out, w_o, w_pg, w_pe)

    xp, xs = x_prompt, _time_major(x_sample)
    zeros_a = jnp.zeros(((CONV_A_WIDTH - 1) * n_p, WIDTH), f32)
    zeros_b = jnp.zeros(((CONV_B_WIDTH - 1) * n_p, WIDTH), f32)
    zeros_h = jnp.zeros((n_p, WIDTH), f32)
    outs = {k: [] for k in ("ca_p", "cb_p", "h_p", "ca_s", "cb_s", "h_s")}
    for l in range(depth):
        final = l == depth - 1
        xp, na, nb_, nh, *big_bf16 = _run_layer(xp, p_prompt, zeros_a, zeros_b, zeros_h, vecs, w_ri, big_f32,
                                                nb=n_p, layer=l, final_norm=final, batch_major_io=True)
        outs["ca_p"].append(_batch_major(na, n_p))
        outs["cb_p"].append(_batch_major(nb_, n_p))
        outs["h_p"].append(nh)
        xs, na, nb_, nh = _run_layer(xs, _time_major(p_sample[l]), _time_major(state_conv_a[l]),
                                     _time_major(state_conv_b[l]), state_h[l], vecs, w_ri, big_bf16,
                                     nb=n_s, layer=l, final_norm=final, batch_major_io=False)
        outs["ca_s"].append(_batch_major(na, n_s))
        outs["cb_s"].append(_batch_major(nb_, n_s))
        outs["h_s"].append(nh)
    return (xp, _batch_major(xs, n_s),
            jnp.stack(outs["ca_p"]), jnp.stack(outs["cb_p"]), jnp.stack(outs["h_p"]),
            jnp.stack(outs["ca_s"]), jnp.stack(outs["cb_s"]), jnp.stack(outs["h_s"]))
```
